```python
import jax, jax.numpy as jnp
from jax import lax
import numpy as np

D_MODEL = 2048
BATCH = 1
SEQ = 8192
DEPTH = 4

CHUNK = 64
N_MEM = 256
N_A_LAYERS = DEPTH // 2
N_B_LAYERS = DEPTH - N_A_LAYERS
D_RNN = 3 * D_MODEL // 4
RNN_BLOCK = 128
N_RNN_BLOCKS = D_RNN // RNN_BLOCK
CONV_WIDTH = 4
LRU_C = 8.0
SB_HEAD_DIM = 128
D_SB = 3 * D_MODEL // 4
N_SB_HEADS = D_SB // SB_HEAD_DIM
N_MEM_HEADS = 4
MEM_HEAD_DIM = 128
D_MEM = N_MEM_HEADS * MEM_HEAD_DIM
D_MIX = D_RNN + D_MEM
Q_BLOCK = 128
EPS = 1e-6

kernel_name = "yoco_rglru_stickbreaking_memory_trunk"


def rms_norm(x, g):
    xf = x.astype(jnp.float32)
    y = xf * lax.rsqrt(jnp.mean(xf * xf, axis=-1, keepdims=True) + EPS)
    return (y * g.astype(jnp.float32)).astype(x.dtype)


def memory_attention(q, mem_k, mem_v):
    b, t, _ = q.shape
    qh = q.reshape(b, t, N_MEM_HEADS, MEM_HEAD_DIM)
    kh = mem_k.reshape(b, N_MEM, N_MEM_HEADS, MEM_HEAD_DIM)
    vh = mem_v.reshape(b, N_MEM, N_MEM_HEADS, MEM_HEAD_DIM)
    s = jnp.einsum('bthd,bmhd->bhtm', qh, kh).astype(jnp.float32) * (MEM_HEAD_DIM ** -0.5)
    p = jax.nn.softmax(s, axis=-1).astype(vh.dtype)
    o = jnp.einsum('bhtm,bmhd->bthd', p, vh)
    return o.reshape(b, t, D_MEM)


def causal_depthwise_conv(x, w, bias):
    y = lax.conv_general_dilated(
        x, w[:, None, :], window_strides=(1,), padding=[(CONV_WIDTH - 1, 0)],
        dimension_numbers=('NWC', 'WIO', 'NWC'), feature_group_count=x.shape[-1])
    return y + bias


def rg_lru(x, w_r, b_r, w_i, b_i, lam):
    b, t, _ = x.shape
    xb = x.reshape(b, t, N_RNN_BLOCKS, RNN_BLOCK)
    r = jax.nn.sigmoid(jnp.einsum('btnc,ncd->btnd', xb, w_r).reshape(b, t, D_RNN) + b_r)
    i = jax.nn.sigmoid(jnp.einsum('btnc,ncd->btnd', xb, w_i).reshape(b, t, D_RNN) + b_i)
    log_a = (-LRU_C * jax.nn.softplus(-lam.astype(jnp.float32))) * r.astype(jnp.float32)
    a = jnp.exp(log_a)
    u = jnp.sqrt(-jnp.expm1(2.0 * log_a)) * (i * x).astype(jnp.float32)

    def combine(left, right):
        a1, h1 = left
        a2, h2 = right
        return a1 * a2, a2 * h1 + h2

    _, h = lax.associative_scan(combine, (a, u), axis=1)
    return h.astype(x.dtype)


def stick_breaking_attention(q, k, v):
    b, t, _ = q.shape
    n_blk = t // Q_BLOCK
    qh = q.reshape(b, n_blk, Q_BLOCK, N_SB_HEADS, SB_HEAD_DIM).transpose(1, 0, 3, 2, 4)
    kh = k.reshape(b, t, N_SB_HEADS, SB_HEAD_DIM).transpose(0, 2, 1, 3)
    vh = v.reshape(b, t, N_SB_HEADS, SB_HEAD_DIM).transpose(0, 2, 1, 3)
    key_pos = jnp.arange(t)
    scale = SB_HEAD_DIM ** -0.5

    def one_block(args):
        q_blk, blk = args
        q_pos = blk * Q_BLOCK + jnp.arange(Q_BLOCK)
        mask = key_pos[None, :] < q_pos[:, None]
        z = jnp.einsum('bhqd,bhkd->bhqk', q_blk, kh).astype(jnp.float32) * scale
        log_beta = jax.nn.log_sigmoid(z)
        log_keep = jnp.where(mask, jax.nn.log_sigmoid(-z), 0.0)
        later = lax.cumsum(log_keep, axis=3, reverse=True) - log_keep
        w = jnp.where(mask, jnp.exp(log_beta + later), 0.0)
        return jnp.einsum('bhqk,bhkd->bhqd', w.astype(vh.dtype), vh)

    o = lax.map(one_block, (qh, jnp.arange(n_blk)))
    return o.transpose(1, 0, 3, 2, 4).reshape(b, t, D_SB)


def layer_a(h, g_norm, w_in, conv_w, conv_b, w_r, b_r, w_i, b_i, lam, w_out, mem_k, mem_v):
    u = rms_norm(h, g_norm)
    proj = u @ w_in
    x_rnn, g_rnn, q_mem, g_mem = jnp.split(proj, [D_RNN, 2 * D_RNN, 2 * D_RNN + D_MEM], axis=-1)
    x_rnn = causal_depthwise_conv(x_rnn, conv_w, conv_b)
    y_rnn = rg_lru(x_rnn, w_r, b_r, w_i, b_i, lam) * jax.nn.silu(g_rnn)
    y_mem = memory_attention(q_mem, mem_k, mem_v) * jax.nn.silu(g_mem)
    return h + jnp.concatenate([y_rnn, y_mem], axis=-1) @ w_out


def layer_b(h, g_norm, w_in, w_out, k_sb, v_sb, mem_k, mem_v):
    u = rms_norm(h, g_norm)
    proj = u @ w_in
    q_sb, g_sb, q_mem, g_mem = jnp.split(proj, [D_SB, 2 * D_SB, 2 * D_SB + D_MEM], axis=-1)
    y_sb = stick_breaking_attention(q_sb, k_sb, v_sb) * jax.nn.silu(g_sb)
    y_mem = memory_attention(q_mem, mem_k, mem_v) * jax.nn.silu(g_mem)
    return h + jnp.concatenate([y_sb, y_mem], axis=-1) @ w_out


def setup_inputs(seed: int = 0) -> dict:
    key = jax.random.key(seed)
    ks = jax.random.split(key, 24)
    f32 = jnp.float32

    def normal(k, shape, scale):
        return jax.random.normal(k, shape, f32) * scale

    def gain(k, shape):
        return 1.0 + 0.02 * jax.random.normal(k, shape, f32)

    a_pow = jax.random.uniform(ks[13], (N_A_LAYERS, D_RNN), f32, 0.9, 0.999)
    s = a_pow ** (1.0 / LRU_C)
    lru_lambda = jnp.log(s) - jnp.log1p(-s)
    return {
        "x": normal(ks[0], (BATCH, SEQ, D_MODEL), 1.0),
        "mem": normal(ks[1], (BATCH, N_MEM, D_MODEL), 1.0),
        "mem_norm": gain(ks[2], (D_MODEL,)),
        "w_mem_kv": normal(ks[3], (DEPTH, D_MODEL, 2 * D_MEM), D_MODEL ** -0.5),
        "norm_a": gain(ks[4], (N_A_LAYERS, D_MODEL)),
        "w_in_a": normal(ks[5], (N_A_LAYERS, D_MODEL, 2 * D_RNN + 2 * D_MEM), D_MODEL ** -0.5),
        "conv_w": normal(ks[6], (N_A_LAYERS, CONV_WIDTH, D_RNN), CONV_WIDTH ** -0.5),
        "conv_b": normal(ks[7], (N_A_LAYERS, D_RNN), 0.01),
        "w_rec_gate": normal(ks[8], (N_A_LAYERS, N_RNN_BLOCKS, RNN_BLOCK, RNN_BLOCK), RNN_BLOCK ** -0.5),
        "b_rec_gate": normal(ks[9], (N_A_LAYERS, D_RNN), 0.01),
        "w_in_gate": normal(ks[10], (N_A_LAYERS, N_RNN_BLOCKS, RNN_BLOCK, RNN_BLOCK), RNN_BLOCK ** -0.5),
        "b_in_gate": normal(ks[11], (N_A_LAYERS, D_RNN), 0.01),
        "lru_lambda": lru_lambda,
        "w_out_a": normal(ks[12], (N_A_LAYERS, D_MIX, D_MODEL), D_MIX ** -0.5),
        "kv_norm": gain(ks[14], (D_MODEL,)),
        "w_kv": normal(ks[15], (D_MODEL, 2 * D_SB), D_MODEL ** -0.5),
        "norm_b": gain(ks[16], (N_B_LAYERS, D_MODEL)),
        "w_in_b": normal(ks[17], (N_B_LAYERS, D_MODEL, 2 * D_SB + 2 * D_MEM), D_MODEL ** -0.5),
        "w_out_b": normal(ks[18], (N_B_LAYERS, D_SB + D_MEM, D_MODEL), (D_SB + D_MEM) ** -0.5),
        "final_norm": gain(ks[19], (D_MODEL,)),
    }


def reference(x, mem, mem_norm, w_mem_kv, norm_a, w_in_a, conv_w, conv_b, w_rec_gate, b_rec_gate,
              w_in_gate, b_in_gate, lru_lambda, w_out_a, kv_norm, w_kv, norm_b, w_in_b, w_out_b,
              final_norm):
    mem_n = rms_norm(mem, mem_norm)
    h = x
    k_sb = None
    v_sb = None
    for layer in range(DEPTH):
        mem_k, mem_v = jnp.split(mem_n @ w_mem_kv[layer], 2, axis=-1)
        if layer < N_A_LAYERS:
            h = layer_a(h, norm_a[layer], w_in_a[layer], conv_w[layer], conv_b[layer],
                        w_rec_gate[layer], b_rec_gate[layer], w_in_gate[layer], b_in_gate[layer],
                        lru_lambda[layer], w_out_a[layer], mem_k, mem_v)
            if layer == N_A_LAYERS - 1:
                k_sb, v_sb = jnp.split(rms_norm(h, kv_norm) @ w_kv, 2, axis=-1)
        else:
            j = layer - N_A_LAYERS
            h = layer_b(h, norm_b[j], w_in_b[j], w_out_b[j], k_sb, v_sb, mem_k, mem_v)
    return rms_norm(h, final_norm)
```

```python
import functools

import jax
import jax.numpy as jnp
from jax import lax
from jax.experimental import pallas as pl
from jax.experimental.pallas import tpu as pltpu

F32 = jnp.float32
BF16 = jnp.bfloat16

D_MODEL = 2048
N_MEM = 256
D_RNN = 1536
RNN_BLOCK = 128
N_RNN_BLOCKS = D_RNN // RNN_BLOCK
CONV_WIDTH = 4
LRU_C = 8.0
SB_HEAD_DIM = 128
D_SB = 1536
N_SB_HEADS = D_SB // SB_HEAD_DIM
N_MEM_HEADS = 4
MEM_HEAD_DIM = 128
D_MEM = N_MEM_HEADS * MEM_HEAD_DIM
EPS = 1e-6

SUBLANES = 8

TM_PROJ = 256
TM_OUT = 256
TB_RNN = 512
TQ_SB = 256
TK_SB = 256

VMEM_LIMIT = 56 * 1024 * 1024


def _rms(x, g):
    y = x * lax.rsqrt(jnp.mean(x * x, axis=-1, keepdims=True) + EPS)
    return y * g


def _sigmoid(x):
    return 1.0 / (1.0 + jnp.exp(-x))


def _silu(x):
    return x * _sigmoid(x)


def _dot(a, b):
    return jnp.dot(a, b, preferred_element_type=F32)


def _dot_nt(a, b):
    return lax.dot_general(a, b, (((1,), (1,)), ((), ())), preferred_element_type=F32)


def _resident(shape):
    nd = len(shape)
    return pl.BlockSpec(shape, lambda *_: (0,) * nd, pipeline_mode=pl.Buffered(1))


def _mem_kv_kernel(mem_ref, g_ref, w_ref, o_ref):
    u = _rms(mem_ref[...], g_ref[...]).astype(BF16)
    o_ref[0] = _dot(u, w_ref[0].astype(BF16)).astype(BF16)


def _mem_kv(mem2d, mem_norm, w_mem_kv):
    depth = w_mem_kv.shape[0]
    return pl.pallas_call(
        _mem_kv_kernel,
        grid=(depth,),
        in_specs=[
            pl.BlockSpec((N_MEM, D_MODEL), lambda l: (0, 0)),
            pl.BlockSpec((1, D_MODEL), lambda l: (0, 0)),
            pl.BlockSpec((1, D_MODEL, 2 * D_MEM), lambda l: (l, 0, 0)),
        ],
        out_specs=pl.BlockSpec((1, N_MEM, 2 * D_MEM), lambda l: (l, 0, 0)),
        out_shape=jax.ShapeDtypeStruct((depth, N_MEM, 2 * D_MEM), BF16),
        compiler_params=pltpu.CompilerParams(
            dimension_semantics=("arbitrary",), vmem_limit_bytes=VMEM_LIMIT),
        name="mem_kv",
    )(mem2d, mem_norm.reshape(1, D_MODEL), w_mem_kv)


def _in_proj_kernel(h_ref, g_ref, w_ref, memkv_ref, a_ref, gate_ref, ymem_ref, *, d_a):
    u = _rms(h_ref[...], g_ref[...]).astype(BF16)
    nchunk = 512
    for c in range(d_a // nchunk):
        sl = slice(c * nchunk, (c + 1) * nchunk)
        a_ref[:, sl] = _dot(u, w_ref[:, sl]).astype(a_ref.dtype)
    for c in range(d_a // nchunk):
        sl = slice(c * nchunk, (c + 1) * nchunk)
        gate_ref[:, sl] = _dot(u, w_ref[:, d_a + c * nchunk:d_a + (c + 1) * nchunk])
    q_mem = _dot(u, w_ref[:, 2 * d_a:2 * d_a + D_MEM])
    g_mem = _dot(u, w_ref[:, 2 * d_a + D_MEM:2 * d_a + 2 * D_MEM])
    scale = MEM_HEAD_DIM ** -0.5
    for hd in range(N_MEM_HEADS):
        sl = slice(hd * MEM_HEAD_DIM, (hd + 1) * MEM_HEAD_DIM)
        q = q_mem[:, sl].astype(BF16)
        k = memkv_ref[:, sl]
        v = memkv_ref[:, D_MEM + hd * MEM_HEAD_DIM:D_MEM + (hd + 1) * MEM_HEAD_DIM]
        s = _dot_nt(q, k) * scale
        e = jnp.exp(s - jnp.max(s, axis=-1, keepdims=True))
        p = e * (1.0 / jnp.sum(e, axis=-1, keepdims=True))
        o = _dot(p.astype(BF16), v)
        ymem_ref[:, sl] = (o * _silu(g_mem[:, sl])).astype(ymem_ref.dtype)


def _in_proj(h, g, w_bf16, memkv, *, d_a, a_dtype):
    t = h.shape[0]
    n_out = w_bf16.shape[1]
    return pl.pallas_call(
        functools.partial(_in_proj_kernel, d_a=d_a),
        grid=(t // TM_PROJ,),
        in_specs=[
            pl.BlockSpec((TM_PROJ, D_MODEL), lambda i: (i, 0)),
            _resident((1, D_MODEL)),
            _resident((D_MODEL, n_out)),
            _resident((N_MEM, 2 * D_MEM)),
        ],
        out_specs=[
            pl.BlockSpec((TM_PROJ, d_a), lambda i: (i, 0)),
            pl.BlockSpec((TM_PROJ, d_a), lambda i: (i, 0)),
            pl.BlockSpec((TM_PROJ, D_MEM), lambda i: (i, 0)),
        ],
        out_shape=[
            jax.ShapeDtypeStruct((t, d_a), a_dtype),
            jax.ShapeDtypeStruct((t, d_a), F32),
            jax.ShapeDtypeStruct((t, D_MEM), BF16),
        ],
        compiler_params=pltpu.CompilerParams(
            dimension_semantics=("arbitrary",), vmem_limit_bytes=VMEM_LIMIT),
        name="in_proj",
    )(h, g.reshape(1, D_MODEL), w_bf16, memkv)


def _rglru_kernel(x_ref, g_ref, cw_ref, cb_ref, wg_ref, br_ref, bi_ref, lam_ref, y_ref,
                  xpad_ref, carry_ref):
    tb = x_ref.shape[0]
    groups = tb // SUBLANES

    @pl.when(pl.program_id(1) == 0)
    def _():
        xpad_ref[0:SUBLANES, :] = jnp.zeros((SUBLANES, RNN_BLOCK), F32)
        carry_ref[...] = jnp.zeros_like(carry_ref)

    x = x_ref[...]
    xpad_ref[SUBLANES:, :] = x
    cw = cw_ref[...]
    xc = x * cw[CONV_WIDTH - 1:CONV_WIDTH, :] + cb_ref[...]
    for k in range(1, CONV_WIDTH):
        xc = xc + xpad_ref[SUBLANES - k:SUBLANES - k + tb, :] * cw[CONV_WIDTH - 1 - k:CONV_WIDTH - k, :]
    xpad_ref[0:SUBLANES, :] = x[tb - SUBLANES:, :]

    gz = _dot(xc.astype(BF16), wg_ref[0])
    r = _sigmoid(gz[:, :RNN_BLOCK] + br_ref[...])
    ig = _sigmoid(gz[:, RNN_BLOCK:] + bi_ref[...])
    nlam = -lam_ref[...]
    softplus_nlam = jnp.maximum(nlam, 0.0) + jnp.log1p(jnp.exp(-jnp.abs(nlam)))
    log_a = (-LRU_C * softplus_nlam) * r
    a = jnp.exp(log_a)
    u = jnp.sqrt(1.0 - a * a) * (ig * xc)

    a3 = a.reshape(groups, SUBLANES, RNN_BLOCK)
    u3 = u.reshape(groups, SUBLANES, RNN_BLOCK)
    row = lax.broadcasted_iota(jnp.int32, (groups, SUBLANES, RNN_BLOCK), 1)
    for k in (1, 2, 4):
        a_sh = pltpu.roll(a3, k, axis=1)
        u_sh = pltpu.roll(u3, k, axis=1)
        m = row >= k
        u3 = jnp.where(m, a3 * u_sh + u3, u3)
        a3 = jnp.where(m, a3 * a_sh, a3)

    carry = carry_ref[...]
    gate = g_ref[...]
    for gi in range(groups):
        hg = u3[gi] + a3[gi] * carry
        sl = slice(gi * SUBLANES, (gi + 1) * SUBLANES)
        y_ref[sl, :] = (hg * _silu(gate[sl, :])).astype(y_ref.dtype)
        carry = jnp.broadcast_to(hg[SUBLANES - 1:SUBLANES, :], (SUBLANES, RNN_BLOCK))
    carry_ref[...] = carry


def _rglru(x_rnn, g_rnn, conv_w, conv_b, w_gate_bf16, b_r, b_i, lam):
    t = x_rnn.shape[0]
    row_spec = pl.BlockSpec((1, RNN_BLOCK), lambda n, i: (0, n))
    tile_spec = pl.BlockSpec((TB_RNN, RNN_BLOCK), lambda n, i: (i, n))
    return pl.pallas_call(
        _rglru_kernel,
        grid=(N_RNN_BLOCKS, t // TB_RNN),
        in_specs=[
            tile_spec,
            tile_spec,
            pl.BlockSpec((CONV_WIDTH, RNN_BLOCK), lambda n, i: (0, n)),
            row_spec,
            pl.BlockSpec((1, RNN_BLOCK, 2 * RNN_BLOCK), lambda n, i: (n, 0, 0)),
            row_spec,
            row_spec,
            row_spec,
        ],
        out_specs=tile_spec,
        out_shape=jax.ShapeDtypeStruct((t, D_RNN), BF16),
        scratch_shapes=[
            pltpu.VMEM((TB_RNN + SUBLANES, RNN_BLOCK), F32),
            pltpu.VMEM((SUBLANES, RNN_BLOCK), F32),
        ],
        compiler_params=pltpu.CompilerParams(
            dimension_semantics=("arbitrary", "arbitrary"), vmem_limit_bytes=VMEM_LIMIT),
        name="rglru",
    )(x_rnn, g_rnn, conv_w, conv_b.reshape(1, D_RNN), w_gate_bf16,
      b_r.reshape(1, D_RNN), b_i.reshape(1, D_RNN), lam.reshape(1, D_RNN))


def _sb_block(q, k, v, carry, acc, *, diagonal):
    tq, tk = q.shape[0], k.shape[0]
    z = _dot_nt(q, k) * (SB_HEAD_DIM ** -0.5)
    l = jnp.log1p(jnp.exp(-jnp.abs(z)))
    log_beta = jnp.minimum(z, 0.0) - l
    log_keep = -jnp.maximum(z, 0.0) - l
    if diagonal:
        mask = (lax.broadcasted_iota(jnp.int32, (tq, tk), 1)
                < lax.broadcasted_iota(jnp.int32, (tq, tk), 0))
        log_keep = jnp.where(mask, log_keep, 0.0)
    upper = (lax.broadcasted_iota(jnp.int32, (tk, tk), 0)
             > lax.broadcasted_iota(jnp.int32, (tk, tk), 1)).astype(BF16)
    hi = log_keep.astype(BF16)
    lo = (log_keep - hi.astype(F32)).astype(BF16)
    later = _dot(hi, upper) + _dot(lo, upper)
    w = jnp.exp(log_beta + later + carry)
    if diagonal:
        w = jnp.where(mask, w, 0.0)
    acc = acc + _dot(w.astype(BF16), v)
    carry = carry + jnp.sum(log_keep, axis=-1, keepdims=True)
    return carry, acc


def _sb_attn_kernel(q_ref, g_ref, k_ref, v_ref, o_ref):
    i = pl.program_id(1)
    q = q_ref[...]
    tq = q.shape[0]

    def load_kv(j):
        start = pl.multiple_of(j * TK_SB, TK_SB)
        return k_ref[pl.ds(start, TK_SB), :], v_ref[pl.ds(start, TK_SB), :]

    k, v = load_kv(i)
    carry = jnp.zeros((tq, 1), F32)
    acc = jnp.zeros((tq, SB_HEAD_DIM), F32)
    carry, acc = _sb_block(q, k, v, carry, acc, diagonal=True)

    def body(step, state):
        k, v = load_kv(i - 1 - step)
        return _sb_block(q, k, v, *state, diagonal=False)

    carry, acc = lax.fori_loop(0, i, body, (carry, acc))
    o_ref[...] = (acc * _silu(g_ref[...])).astype(o_ref.dtype)


def _sb_attn(q_sb, g_sb, kv):
    t = q_sb.shape[0]
    assert TQ_SB == TK_SB
    return pl.pallas_call(
        _sb_attn_kernel,
        grid=(N_SB_HEADS, t // TQ_SB),
        in_specs=[
            pl.BlockSpec((TQ_SB, SB_HEAD_DIM), lambda h, i: (i, h)),
            pl.BlockSpec((TQ_SB, SB_HEAD_DIM), lambda h, i: (i, h)),
            pl.BlockSpec((t, SB_HEAD_DIM), lambda h, i: (0, h)),
            pl.BlockSpec((t, SB_HEAD_DIM), lambda h, i: (0, N_SB_HEADS + h)),
        ],
        out_specs=pl.BlockSpec((TQ_SB, SB_HEAD_DIM), lambda h, i: (i, h)),
        out_shape=jax.ShapeDtypeStruct((t, D_SB), BF16),
        compiler_params=pltpu.CompilerParams(
            dimension_semantics=("arbitrary", "arbitrary"), vmem_limit_bytes=VMEM_LIMIT),
        name="sb_attn",
    )(q_sb, g_sb, kv, kv)


def _out_proj_kernel(ya_ref, ym_ref, h_ref, w_ref, gf_ref, o_ref, *, d_a, final_norm):
    out = _dot(ya_ref[...], w_ref[0:d_a, :]) + _dot(ym_ref[...], w_ref[d_a:, :]) + h_ref[...]
    if final_norm:
        out = _rms(out, gf_ref[...])
    o_ref[...] = out


def _out_proj(ya, ym, h, w_bf16, g_final, *, final_norm):
    t, d_a = ya.shape
    return pl.pallas_call(
        functools.partial(_out_proj_kernel, d_a=d_a, final_norm=final_norm),
        grid=(t // TM_OUT,),
        in_specs=[
            pl.BlockSpec((TM_OUT, d_a), lambda i: (i, 0)),
            pl.BlockSpec((TM_OUT, D_MEM), lambda i: (i, 0)),
            pl.BlockSpec((TM_OUT, D_MODEL), lambda i: (i, 0)),
            _resident((d_a + D_MEM, D_MODEL)),
            _resident((1, D_MODEL)),
        ],
        out_specs=pl.BlockSpec((TM_OUT, D_MODEL), lambda i: (i, 0)),
        out_shape=jax.ShapeDtypeStruct((t, D_MODEL), F32),
        compiler_params=pltpu.CompilerParams(
            dimension_semantics=("arbitrary",), vmem_limit_bytes=VMEM_LIMIT),
        name="out_proj",
    )(ya, ym, h, w_bf16, g_final.reshape(1, D_MODEL))


def _norm_matmul_kernel(h_ref, g_ref, w_ref, o_ref):
    u = _rms(h_ref[...], g_ref[...]).astype(BF16)
    nchunk = 512
    for c in range(o_ref.shape[1] // nchunk):
        sl = slice(c * nchunk, (c + 1) * nchunk)
        o_ref[:, sl] = _dot(u, w_ref[:, sl]).astype(o_ref.dtype)


def _norm_matmul(h, g, w_bf16):
    t = h.shape[0]
    n_out = w_bf16.shape[1]
    return pl.pallas_call(
        _norm_matmul_kernel,
        grid=(t // TM_PROJ,),
        in_specs=[
            pl.BlockSpec((TM_PROJ, D_MODEL), lambda i: (i, 0)),
            _resident((1, D_MODEL)),
            _resident((D_MODEL, n_out)),
        ],
        out_specs=pl.BlockSpec((TM_PROJ, n_out), lambda i: (i, 0)),
        out_shape=jax.ShapeDtypeStruct((t, n_out), BF16),
        compiler_params=pltpu.CompilerParams(
            dimension_semantics=("arbitrary",), vmem_limit_bytes=VMEM_LIMIT),
        name="kv_proj",
    )(h, g.reshape(1, D_MODEL), w_bf16)


def kernel(x, mem, mem_norm, w_mem_kv, norm_a, w_in_a, conv_w, conv_b, w_rec_gate, b_rec_gate,
           w_in_gate, b_in_gate, lru_lambda, w_out_a, kv_norm, w_kv, norm_b, w_in_b, w_out_b,
           final_norm):
    batch, seq, _ = x.shape
    n_a = w_in_a.shape[0]
    n_b = w_in_b.shape[0]
    outs = []
    for b in range(batch):
        memkv = _mem_kv(mem[b], mem_norm, w_mem_kv)
        h = x[b]
        for layer in range(n_a):
            x_rnn, g_rnn, y_mem = _in_proj(h, norm_a[layer], w_in_a[layer].astype(BF16),
                                           memkv[layer], d_a=D_RNN, a_dtype=F32)
            w_gate = jnp.concatenate([w_rec_gate[layer], w_in_gate[layer]], axis=-1).astype(BF16)
            y_rnn = _rglru(x_rnn, g_rnn, conv_w[layer], conv_b[layer], w_gate,
                           b_rec_gate[layer], b_in_gate[layer], lru_lambda[layer])
            h = _out_proj(y_rnn, y_mem, h, w_out_a[layer].astype(BF16), final_norm,
                          final_norm=False)
        kv = _norm_matmul(h, kv_norm, w_kv.astype(BF16))
        for j in range(n_b):
            q_sb, g_sb, y_mem = _in_proj(h, norm_b[j], w_in_b[j].astype(BF16),
                                         memkv[n_a + j], d_a=D_SB, a_dtype=BF16)
            y_sb = _sb_attn(q_sb, g_sb, kv)
            h = _out_proj(y_sb, y_mem, h, w_out_b[j].astype(BF16), final_norm,
                          final_norm=(j == n_b - 1))
        outs.append(h)
    return jnp.stack(outs, axis=0)
```

```python
import functools

import jax
import jax.numpy as jnp
from jax import lax
from jax.experimental import pallas as pl
from jax.experimental.pallas import tpu as pltpu

F32 = jnp.float32
BF16 = jnp.bfloat16

D_MODEL = 2048
N_MEM = 256
D_RNN = 1536
RNN_BLOCK = 128
N_RNN_BLOCKS = D_RNN // RNN_BLOCK
CONV_WIDTH = 4
LRU_C = 8.0
SB_HEAD_DIM = 128
D_SB = 1536
N_SB_HEADS = D_SB // SB_HEAD_DIM
N_MEM_HEADS = 4
MEM_HEAD_DIM = 128
D_MEM = N_MEM_HEADS * MEM_HEAD_DIM
EPS = 1e-6

SUBLANES = 8

TM_PROJ = 256
TM_OUT = 256
TB_RNN = 512
TQ_SB = 256
TK_SB = 256

VMEM_LIMIT = 56 * 1024 * 1024


def _rms(x, g):
    y = x * lax.rsqrt(jnp.mean(x * x, axis=-1, keepdims=True) + EPS)
    return y * g


def _sigmoid(x):
    return 0.5 * jnp.tanh(0.5 * x) + 0.5


def _silu(x):
    return x * _sigmoid(x)


def _dot(a, b):
    return jnp.dot(a, b, preferred_element_type=F32)


def _dot_nt(a, b):
    return lax.dot_general(a, b, (((1,), (1,)), ((), ())), preferred_element_type=F32)


def _resident(shape):
    nd = len(shape)
    return pl.BlockSpec(shape, lambda *_: (0,) * nd, pipeline_mode=pl.Buffered(1))


def _mem_kv_kernel(mem_ref, g_ref, w_ref, o_ref):
    u = _rms(mem_ref[...], g_ref[...]).astype(BF16)
    o_ref[0] = _dot(u, w_ref[0].astype(BF16)).astype(BF16)


def _mem_kv(mem2d, mem_norm, w_mem_kv):
    depth = w_mem_kv.shape[0]
    return pl.pallas_call(
        _mem_kv_kernel,
        grid=(depth,),
        in_specs=[
            pl.BlockSpec((N_MEM, D_MODEL), lambda l: (0, 0)),
            pl.BlockSpec((1, D_MODEL), lambda l: (0, 0)),
            pl.BlockSpec((1, D_MODEL, 2 * D_MEM), lambda l: (l, 0, 0)),
        ],
        out_specs=pl.BlockSpec((1, N_MEM, 2 * D_MEM), lambda l: (l, 0, 0)),
        out_shape=jax.ShapeDtypeStruct((depth, N_MEM, 2 * D_MEM), BF16),
        compiler_params=pltpu.CompilerParams(
            dimension_semantics=("arbitrary",), vmem_limit_bytes=VMEM_LIMIT),
        name="mem_kv",
    )(mem2d, mem_norm.reshape(1, D_MODEL), w_mem_kv)


def _in_proj_kernel(h_ref, g_ref, w_ref, memkv_ref, a_ref, gate_ref, ymem_ref, *, d_a):
    u = _rms(h_ref[...], g_ref[...]).astype(BF16)
    nchunk = 512
    for c in range(d_a // nchunk):
        sl = slice(c * nchunk, (c + 1) * nchunk)
        a_ref[:, sl] = _dot(u, w_ref[:, sl]).astype(a_ref.dtype)
    for c in range(d_a // nchunk):
        sl = slice(c * nchunk, (c + 1) * nchunk)
        gate_ref[:, sl] = _dot(u, w_ref[:, d_a + c * nchunk:d_a + (c + 1) * nchunk])
    q_mem = _dot(u, w_ref[:, 2 * d_a:2 * d_a + D_MEM])
    g_mem = _dot(u, w_ref[:, 2 * d_a + D_MEM:2 * d_a + 2 * D_MEM])
    scale = MEM_HEAD_DIM ** -0.5
    for hd in range(N_MEM_HEADS):
        sl = slice(hd * MEM_HEAD_DIM, (hd + 1) * MEM_HEAD_DIM)
        q = q_mem[:, sl].astype(BF16)
        k = memkv_ref[:, sl]
        v = memkv_ref[:, D_MEM + hd * MEM_HEAD_DIM:D_MEM + (hd + 1) * MEM_HEAD_DIM]
        s = _dot_nt(q, k) * scale
        e = jnp.exp(s - jnp.max(s, axis=-1, keepdims=True))
        p = e * (1.0 / jnp.sum(e, axis=-1, keepdims=True))
        o = _dot(p.astype(BF16), v)
        ymem_ref[:, sl] = (o * _silu(g_mem[:, sl])).astype(ymem_ref.dtype)


def _in_proj(h, g, w_bf16, memkv, *, d_a, a_dtype):
    t = h.shape[0]
    n_out = w_bf16.shape[1]
    return pl.pallas_call(
        functools.partial(_in_proj_kernel, d_a=d_a),
        grid=(t // TM_PROJ,),
        in_specs=[
            pl.BlockSpec((TM_PROJ, D_MODEL), lambda i: (i, 0)),
            _resident((1, D_MODEL)),
            _resident((D_MODEL, n_out)),
            _resident((N_MEM, 2 * D_MEM)),
        ],
        out_specs=[
            pl.BlockSpec((TM_PROJ, d_a), lambda i: (i, 0)),
            pl.BlockSpec((TM_PROJ, d_a), lambda i: (i, 0)),
            pl.BlockSpec((TM_PROJ, D_MEM), lambda i: (i, 0)),
        ],
        out_shape=[
            jax.ShapeDtypeStruct((t, d_a), a_dtype),
            jax.ShapeDtypeStruct((t, d_a), F32),
            jax.ShapeDtypeStruct((t, D_MEM), BF16),
        ],
        compiler_params=pltpu.CompilerParams(
            dimension_semantics=("arbitrary",), vmem_limit_bytes=VMEM_LIMIT),
        name="in_proj",
    )(h, g.reshape(1, D_MODEL), w_bf16, memkv)


def _rglru_kernel(x_ref, g_ref, cw_ref, cb_ref, wg_ref, br_ref, bi_ref, lam_ref, y_ref,
                  xpad_ref, carry_ref):
    tb = x_ref.shape[0]
    groups = tb // SUBLANES

    @pl.when(pl.program_id(1) == 0)
    def _():
        xpad_ref[0:SUBLANES, :] = jnp.zeros((SUBLANES, RNN_BLOCK), F32)
        carry_ref[...] = jnp.zeros_like(carry_ref)

    x = x_ref[...]
    xpad_ref[SUBLANES:, :] = x
    cw = cw_ref[...]
    xc = x * cw[CONV_WIDTH - 1:CONV_WIDTH, :] + cb_ref[...]
    for k in range(1, CONV_WIDTH):
        xc = xc + xpad_ref[SUBLANES - k:SUBLANES - k + tb, :] * cw[CONV_WIDTH - 1 - k:CONV_WIDTH - k, :]
    xpad_ref[0:SUBLANES, :] = x[tb - SUBLANES:, :]

    gz = _dot(xc.astype(BF16), wg_ref[0])
    r = _sigmoid(gz[:, :RNN_BLOCK] + br_ref[...])
    ig = _sigmoid(gz[:, RNN_BLOCK:] + bi_ref[...])
    nlam = -lam_ref[...]
    softplus_nlam = jnp.maximum(nlam, 0.0) + jnp.log1p(jnp.exp(-jnp.abs(nlam)))
    log_a = (-LRU_C * softplus_nlam) * r
    a = jnp.exp(log_a)
    u = jnp.sqrt(1.0 - a * a) * (ig * xc)

    a3 = a.reshape(groups, SUBLANES, RNN_BLOCK)
    u3 = u.reshape(groups, SUBLANES, RNN_BLOCK)
    row = lax.broadcasted_iota(jnp.int32, (groups, SUBLANES, RNN_BLOCK), 1)
    for k in (1, 2, 4):
        a_sh = pltpu.roll(a3, k, axis=1)
        u_sh = pltpu.roll(u3, k, axis=1)
        m = row >= k
        u3 = jnp.where(m, a3 * u_sh + u3, u3)
        a3 = jnp.where(m, a3 * a_sh, a3)

    carry = carry_ref[...]
    gate = g_ref[...]
    for gi in range(groups):
        hg = u3[gi] + a3[gi] * carry
        sl = slice(gi * SUBLANES, (gi + 1) * SUBLANES)
        y_ref[sl, :] = (hg * _silu(gate[sl, :])).astype(y_ref.dtype)
        carry = jnp.broadcast_to(hg[SUBLANES - 1:SUBLANES, :], (SUBLANES, RNN_BLOCK))
    carry_ref[...] = carry


def _rglru(x_rnn, g_rnn, conv_w, conv_b, w_gate_bf16, b_r, b_i, lam):
    t = x_rnn.shape[0]
    row_spec = pl.BlockSpec((1, RNN_BLOCK), lambda n, i: (0, n))
    tile_spec = pl.BlockSpec((TB_RNN, RNN_BLOCK), lambda n, i: (i, n))
    return pl.pallas_call(
        _rglru_kernel,
        grid=(N_RNN_BLOCKS, t // TB_RNN),
        in_specs=[
            tile_spec,
            tile_spec,
            pl.BlockSpec((CONV_WIDTH, RNN_BLOCK), lambda n, i: (0, n)),
            row_spec,
            pl.BlockSpec((1, RNN_BLOCK, 2 * RNN_BLOCK), lambda n, i: (n, 0, 0)),
            row_spec,
            row_spec,
            row_spec,
        ],
        out_specs=tile_spec,
        out_shape=jax.ShapeDtypeStruct((t, D_RNN), BF16),
        scratch_shapes=[
            pltpu.VMEM((TB_RNN + SUBLANES, RNN_BLOCK), F32),
            pltpu.VMEM((SUBLANES, RNN_BLOCK), F32),
        ],
        compiler_params=pltpu.CompilerParams(
            dimension_semantics=("arbitrary", "arbitrary"), vmem_limit_bytes=VMEM_LIMIT),
        name="rglru",
    )(x_rnn, g_rnn, conv_w, conv_b.reshape(1, D_RNN), w_gate_bf16,
      b_r.reshape(1, D_RNN), b_i.reshape(1, D_RNN), lam.reshape(1, D_RNN))


LOG2E = 1.4426950408889634
SB_EXIT_LOG2 = -104.0 * LOG2E
SB_HEADS_PER_STEP = 2


def _sb_block(q, k, v, upper2, carry, acc, *, diagonal, valid=None):
    tq, tk = q.shape[0], k.shape[0]
    z = _dot_nt(q, k) * (SB_HEAD_DIM ** -0.5 * LOG2E)
    l = jnp.log(1.0 + jnp.exp2(-jnp.abs(z))) * LOG2E
    log_beta = jnp.minimum(z, 0.0) - l
    log_keep = log_beta - z
    if diagonal:
        mask = (lax.broadcasted_iota(jnp.int32, (tq, tk), 1)
                < lax.broadcasted_iota(jnp.int32, (tq, tk), 0))
        log_keep = jnp.where(mask, log_keep, 0.0)
    hi = log_keep.astype(BF16)
    lo = (log_keep - hi.astype(F32)).astype(BF16)
    later = _dot(jnp.concatenate([hi, lo], axis=1), upper2)
    x = log_beta + later
    if carry is not None:
        x = x + carry
    w = jnp.exp2(x)
    if diagonal:
        w = jnp.where(mask, w, 0.0)
    if valid is not None:
        w = jnp.where(valid, w, 0.0)
    acc = acc + _dot(w.astype(BF16), v)
    total = jnp.sum(log_keep, axis=-1, keepdims=True)
    carry = total if carry is None else carry + total
    return carry, acc


def _sb_attn_kernel(q_ref, g_ref, k_ref, v_ref, o_ref):
    i = pl.program_id(1)
    tq = q_ref.shape[0]
    heads = range(SB_HEADS_PER_STEP)
    lanes = [slice(hd * SB_HEAD_DIM, (hd + 1) * SB_HEAD_DIM) for hd in heads]
    upper = (lax.broadcasted_iota(jnp.int32, (TK_SB, TK_SB), 0)
             > lax.broadcasted_iota(jnp.int32, (TK_SB, TK_SB), 1)).astype(BF16)
    upper2 = jnp.concatenate([upper, upper], axis=0)
    qs = [q_ref[:, sl] for sl in lanes]

    def load_kv(j, hd):
        start = pl.multiple_of(j * TK_SB, TK_SB)
        return k_ref[pl.ds(start, TK_SB), lanes[hd]], v_ref[pl.ds(start, TK_SB), lanes[hd]]

    def needs_more(carries):
        worst = functools.reduce(jnp.maximum, carries)
        return (jnp.max(worst) >= SB_EXIT_LOG2).astype(jnp.int32)

    carries, accs = [], []
    for hd in heads:
        k, v = load_kv(i, hd)
        carry, acc = _sb_block(qs[hd], k, v, upper2, None,
                               jnp.zeros((tq, SB_HEAD_DIM), F32), diagonal=True)
        carries.append(carry)
        accs.append(acc)

    has_prev = i > 0
    for hd in heads:
        k, v = load_kv(jnp.maximum(i - 1, 0), hd)
        carries[hd], accs[hd] = _sb_block(qs[hd], k, v, upper2, carries[hd], accs[hd],
                                          diagonal=False, valid=has_prev)

    def cond(state):
        return jnp.logical_and(state[0] >= 0, state[1] > 0)

    def body(state):
        j = state[0]
        carries, accs = list(state[2]), list(state[3])
        for hd in heads:
            k, v = load_kv(j, hd)
            carries[hd], accs[hd] = _sb_block(qs[hd], k, v, upper2, carries[hd], accs[hd],
                                              diagonal=False)
        return j - 1, needs_more(carries), tuple(carries), tuple(accs)

    state = lax.while_loop(cond, body,
                           (i - 2, needs_more(carries), tuple(carries), tuple(accs)))
    accs = state[3]
    for hd in heads:
        o_ref[:, lanes[hd]] = (accs[hd] * _silu(g_ref[:, lanes[hd]])).astype(o_ref.dtype)


def _sb_attn(q_sb, g_sb, kv):
    t = q_sb.shape[0]
    assert TQ_SB == TK_SB
    width = SB_HEADS_PER_STEP * SB_HEAD_DIM
    groups = N_SB_HEADS // SB_HEADS_PER_STEP
    tile_spec = pl.BlockSpec((TQ_SB, width), lambda h, i: (i, h))
    return pl.pallas_call(
        _sb_attn_kernel,
        grid=(groups, t // TQ_SB),
        in_specs=[
            tile_spec,
            tile_spec,
            pl.BlockSpec((t, width), lambda h, i: (0, h)),
            pl.BlockSpec((t, width), lambda h, i: (0, groups + h)),
        ],
        out_specs=tile_spec,
        out_shape=jax.ShapeDtypeStruct((t, D_SB), BF16),
        compiler_params=pltpu.CompilerParams(
            dimension_semantics=("arbitrary", "arbitrary"), vmem_limit_bytes=VMEM_LIMIT),
        name="sb_attn",
    )(q_sb, g_sb, kv, kv)


def _out_proj_kernel(ya_ref, ym_ref, h_ref, w_ref, gf_ref, o_ref, *, d_a, final_norm):
    out = _dot(ya_ref[...], w_ref[0:d_a, :]) + _dot(ym_ref[...], w_ref[d_a:, :]) + h_ref[...]
    if final_norm:
        out = _rms(out, gf_ref[...])
    o_ref[...] = out


def _out_proj(ya, ym, h, w_bf16, g_final, *, final_norm):
    t, d_a = ya.shape
    return pl.pallas_call(
        functools.partial(_out_proj_kernel, d_a=d_a, final_norm=final_norm),
        grid=(t // TM_OUT,),
        in_specs=[
            pl.BlockSpec((TM_OUT, d_a), lambda i: (i, 0)),
            pl.BlockSpec((TM_OUT, D_MEM), lambda i: (i, 0)),
            pl.BlockSpec((TM_OUT, D_MODEL), lambda i: (i, 0)),
            _resident((d_a + D_MEM, D_MODEL)),
            _resident((1, D_MODEL)),
        ],
        out_specs=pl.BlockSpec((TM_OUT, D_MODEL), lambda i: (i, 0)),
        out_shape=jax.ShapeDtypeStruct((t, D_MODEL), F32),
        compiler_params=pltpu.CompilerParams(
            dimension_semantics=("arbitrary",), vmem_limit_bytes=VMEM_LIMIT),
        name="out_proj",
    )(ya, ym, h, w_bf16, g_final.reshape(1, D_MODEL))


def _norm_matmul_kernel(h_ref, g_ref, w_ref, o_ref):
    u = _rms(h_ref[...], g_ref[...]).astype(BF16)
    nchunk = 512
    for c in range(o_ref.shape[1] // nchunk):
        sl = slice(c * nchunk, (c + 1) * nchunk)
        o_ref[:, sl] = _dot(u, w_ref[:, sl]).astype(o_ref.dtype)


def _norm_matmul(h, g, w_bf16):
    t = h.shape[0]
    n_out = w_bf16.shape[1]
    return pl.pallas_call(
        _norm_matmul_kernel,
        grid=(t // TM_PROJ,),
        in_specs=[
            pl.BlockSpec((TM_PROJ, D_MODEL), lambda i: (i, 0)),
            _resident((1, D_MODEL)),
            _resident((D_MODEL, n_out)),
        ],
        out_specs=pl.BlockSpec((TM_PROJ, n_out), lambda i: (i, 0)),
        out_shape=jax.ShapeDtypeStruct((t, n_out), BF16),
        compiler_params=pltpu.CompilerParams(
            dimension_semantics=("arbitrary",), vmem_limit_bytes=VMEM_LIMIT),
        name="kv_proj",
    )(h, g.reshape(1, D_MODEL), w_bf16)


def kernel(x, mem, mem_norm, w_mem_kv, norm_a, w_in_a, conv_w, conv_b, w_rec_gate, b_rec_gate,
           w_in_gate, b_in_gate, lru_lambda, w_out_a, kv_norm, w_kv, norm_b, w_in_b, w_out_b,
           final_norm):
    batch, seq, _ = x.shape
    n_a = w_in_a.shape[0]
    n_b = w_in_b.shape[0]
    outs = []
    for b in range(batch):
        memkv = _mem_kv(mem[b], mem_norm, w_mem_kv)
        h = x[b]
        for layer in range(n_a):
            x_rnn, g_rnn, y_mem = _in_proj(h, norm_a[layer], w_in_a[layer].astype(BF16),
                                           memkv[layer], d_a=D_RNN, a_dtype=F32)
            w_gate = jnp.concatenate([w_rec_gate[layer], w_in_gate[layer]], axis=-1).astype(BF16)
            y_rnn = _rglru(x_rnn, g_rnn, conv_w[layer], conv_b[layer], w_gate,
                           b_rec_gate[layer], b_in_gate[layer], lru_lambda[layer])
            h = _out_proj(y_rnn, y_mem, h, w_out_a[layer].astype(BF16), final_norm,
                          final_norm=False)
        kv = _norm_matmul(h, kv_norm, w_kv.astype(BF16))
        for j in range(n_b):
            q_sb, g_sb, y_mem = _in_proj(h, norm_b[j], w_in_b[j].astype(BF16),
                                         memkv[n_a + j], d_a=D_SB, a_dtype=BF16)
            y_sb = _sb_attn(q_sb, g_sb, kv)
            h = _out_proj(y_sb, y_mem, h, w_out_b[j].astype(BF16), final_norm,
                          final_norm=(j == n_b - 1))
        outs.append(h)
    return jnp.stack(outs, axis=0)
```

```python
import functools

import jax
import jax.numpy as jnp
from jax import lax
from jax.experimental import pallas as pl
from jax.experimental.pallas import tpu as pltpu

F32 = jnp.float32
BF16 = jnp.bfloat16

D_MODEL = 2048
N_MEM = 256
D_RNN = 1536
RNN_BLOCK = 128
N_RNN_BLOCKS = D_RNN // RNN_BLOCK
CONV_WIDTH = 4
LRU_C = 8.0
SB_HEAD_DIM = 128
D_SB = 1536
N_SB_HEADS = D_SB // SB_HEAD_DIM
N_MEM_HEADS = 4
MEM_HEAD_DIM = 128
D_MEM = N_MEM_HEADS * MEM_HEAD_DIM
EPS = 1e-6

SUBLANES = 8
LANES = 128

TM_PROJ = 256
TM_OUT = 256
TB_RNN = 512
TQ_SB = 256
TK_SB = 256
SB_ROWS = 256
W_CHUNK = 512

VMEM_LIMIT = 56 * 1024 * 1024


def _rms(x, g):
    y = x * lax.rsqrt(jnp.mean(x * x, axis=-1, keepdims=True) + EPS)
    return y * g


def _sigmoid(x):
    return 0.5 * jnp.tanh(0.5 * x) + 0.5


def _silu(x):
    return x * _sigmoid(x)


def _dot(a, b):
    return jnp.dot(a, b, preferred_element_type=F32)


def _dot_nt(a, b):
    return lax.dot_general(a, b, (((1,), (1,)), ((), ())), preferred_element_type=F32)


def _const_spec(shape, index):
    return pl.BlockSpec(shape, lambda *_: index, pipeline_mode=pl.Buffered(1))


def _params(n_axes):
    return pltpu.CompilerParams(dimension_semantics=("arbitrary",) * n_axes,
                                vmem_limit_bytes=VMEM_LIMIT)


def _mem_kv_kernel(mem_ref, g_ref, w_ref, o_ref):
    u = _rms(mem_ref[...], g_ref[...]).astype(BF16)
    o_ref[0] = _dot(u, w_ref[0].astype(BF16)).astype(BF16)


def _mem_kv(mem2d, mem_norm, w_mem_kv):
    depth = w_mem_kv.shape[0]
    return pl.pallas_call(
        _mem_kv_kernel,
        grid=(depth,),
        in_specs=[
            pl.BlockSpec((N_MEM, D_MODEL), lambda l: (0, 0)),
            pl.BlockSpec((1, D_MODEL), lambda l: (0, 0)),
            pl.BlockSpec((1, D_MODEL, 2 * D_MEM), lambda l: (l, 0, 0)),
        ],
        out_specs=pl.BlockSpec((1, N_MEM, 2 * D_MEM), lambda l: (l, 0, 0)),
        out_shape=jax.ShapeDtypeStruct((depth, N_MEM, 2 * D_MEM), BF16),
        compiler_params=_params(1),
        name="mem_kv",
    )(mem2d, mem_norm.reshape(1, D_MODEL), w_mem_kv)


def _store_cols(ref, c, val, slab):
    if slab:
        per = W_CHUNK // LANES
        for k in range(per):
            ref[c * per + k] = val[:, k * LANES:(k + 1) * LANES].astype(ref.dtype)
    else:
        ref[:, c * W_CHUNK:(c + 1) * W_CHUNK] = val.astype(ref.dtype)


def _in_proj_kernel(h_ref, g_ref, w_ref, memkv_ref, a_ref, gate_ref, ymem_ref, w_scr,
                    *, d_a, slab, n_wchunks):
    i = pl.program_id(0)

    @pl.when(i < n_wchunks)
    def _():
        w_scr[i] = w_ref[0].astype(BF16)

    @pl.when(i >= n_wchunks)
    def _():
        u = _rms(h_ref[...], g_ref[0]).astype(BF16)
        na = d_a // W_CHUNK
        for c in range(na):
            _store_cols(a_ref, c, _dot(u, w_scr[c]), slab)
        for c in range(na):
            _store_cols(gate_ref, c, _dot(u, w_scr[na + c]), slab)
        q_mem = _dot(u, w_scr[2 * na])
        g_mem = _dot(u, w_scr[2 * na + 1])
        scale = MEM_HEAD_DIM ** -0.5
        for hd in range(N_MEM_HEADS):
            sl = slice(hd * MEM_HEAD_DIM, (hd + 1) * MEM_HEAD_DIM)
            q = q_mem[:, sl].astype(BF16)
            k = memkv_ref[0, :, sl]
            v = memkv_ref[0, :, D_MEM + hd * MEM_HEAD_DIM:D_MEM + (hd + 1) * MEM_HEAD_DIM]
            s = _dot_nt(q, k) * scale
            e = jnp.exp(s - jnp.max(s, axis=-1, keepdims=True))
            p = e * (1.0 / jnp.sum(e, axis=-1, keepdims=True))
            o = _dot(p.astype(BF16), v)
            ymem_ref[:, sl] = (o * _silu(g_mem[:, sl])).astype(ymem_ref.dtype)


def _in_proj(h, norms, w_in, memkv, layer, mem_layer, *, d_a, a_dtype, slab):
    t = h.shape[0]
    n_out = w_in.shape[2]
    assert d_a % W_CHUNK == 0 and D_MEM == W_CHUNK and n_out == 2 * d_a + 2 * D_MEM
    n_wchunks = n_out // W_CHUNK

    def tile(i):
        return jnp.maximum(i - n_wchunks, 0)

    if slab:
        a_shape, a_block = (d_a // LANES, t, LANES), (d_a // LANES, TM_PROJ, LANES)
        a_spec = pl.BlockSpec(a_block, lambda i: (0, tile(i), 0))
    else:
        a_shape = (t, d_a)
        a_spec = pl.BlockSpec((TM_PROJ, d_a), lambda i: (tile(i), 0))
    return pl.pallas_call(
        functools.partial(_in_proj_kernel, d_a=d_a, slab=slab, n_wchunks=n_wchunks),
        grid=(n_wchunks + t // TM_PROJ,),
        in_specs=[
            pl.BlockSpec((TM_PROJ, D_MODEL), lambda i: (tile(i), 0)),
            _const_spec((1, 1, D_MODEL), (layer, 0, 0)),
            pl.BlockSpec((1, D_MODEL, W_CHUNK),
                         lambda i: (layer, 0, jnp.minimum(i, n_wchunks - 1))),
            _const_spec((1, N_MEM, 2 * D_MEM), (mem_layer, 0, 0)),
        ],
        out_specs=[
            a_spec,
            a_spec,
            pl.BlockSpec((TM_PROJ, D_MEM), lambda i: (tile(i), 0)),
        ],
        out_shape=[
            jax.ShapeDtypeStruct(a_shape, a_dtype),
            jax.ShapeDtypeStruct(a_shape, F32),
            jax.ShapeDtypeStruct((t, D_MEM), BF16),
        ],
        scratch_shapes=[pltpu.VMEM((n_wchunks, D_MODEL, W_CHUNK), BF16)],
        compiler_params=_params(1),
        name="in_proj",
    )(h, norms, w_in, memkv)


def _rglru_kernel(x_ref, g_ref, cw_ref, cb_ref, wr_ref, wi_ref, br_ref, bi_ref, lam_ref, y_ref,
                  xpad_ref, carry_ref):
    tb = x_ref.shape[1]
    groups = tb // SUBLANES

    @pl.when(pl.program_id(1) == 0)
    def _():
        xpad_ref[0:SUBLANES, :] = jnp.zeros((SUBLANES, RNN_BLOCK), F32)
        carry_ref[...] = jnp.zeros_like(carry_ref)

    x = x_ref[0]
    xpad_ref[SUBLANES:, :] = x
    cw = cw_ref[0]
    xc = x * cw[CONV_WIDTH - 1:CONV_WIDTH, :] + cb_ref[0]
    for k in range(1, CONV_WIDTH):
        xc = xc + xpad_ref[SUBLANES - k:SUBLANES - k + tb, :] * cw[CONV_WIDTH - 1 - k:CONV_WIDTH - k, :]
    xpad_ref[0:SUBLANES, :] = x[tb - SUBLANES:, :]

    xb = xc.astype(BF16)
    r = _sigmoid(_dot(xb, wr_ref[0, 0].astype(BF16)) + br_ref[0])
    ig = _sigmoid(_dot(xb, wi_ref[0, 0].astype(BF16)) + bi_ref[0])
    nlam = -lam_ref[0]
    softplus_nlam = jnp.maximum(nlam, 0.0) + jnp.log1p(jnp.exp(-jnp.abs(nlam)))
    log_a = (-LRU_C * softplus_nlam) * r
    a = jnp.exp(log_a)
    u = jnp.sqrt(1.0 - a * a) * (ig * xc)

    a3 = a.reshape(groups, SUBLANES, RNN_BLOCK)
    u3 = u.reshape(groups, SUBLANES, RNN_BLOCK)
    row = lax.broadcasted_iota(jnp.int32, (groups, SUBLANES, RNN_BLOCK), 1)
    for k in (1, 2, 4):
        a_sh = pltpu.roll(a3, k, axis=1)
        u_sh = pltpu.roll(u3, k, axis=1)
        m = row >= k
        u3 = jnp.where(m, a3 * u_sh + u3, u3)
        a3 = jnp.where(m, a3 * a_sh, a3)

    carry = carry_ref[...]
    gate = g_ref[0]
    for gi in range(groups):
        hg = u3[gi] + a3[gi] * carry
        sl = slice(gi * SUBLANES, (gi + 1) * SUBLANES)
        y_ref[0, sl, :] = (hg * _silu(gate[sl, :])).astype(y_ref.dtype)
        carry = jnp.broadcast_to(hg[SUBLANES - 1:SUBLANES, :], (SUBLANES, RNN_BLOCK))
    carry_ref[...] = carry


def _rglru(x_rnn, g_rnn, conv_w, conv_b, w_r, w_i, b_r, b_i, lam, layer):
    t = x_rnn.shape[1]
    n_layers = conv_w.shape[0]
    tile_spec = pl.BlockSpec((1, TB_RNN, RNN_BLOCK), lambda n, i: (n, i, 0))
    row_spec = pl.BlockSpec((1, 1, RNN_BLOCK), lambda n, i: (layer, 0, n))
    gate_w_spec = pl.BlockSpec((1, 1, RNN_BLOCK, RNN_BLOCK), lambda n, i: (layer, n, 0, 0))

    def rows(p):
        return p.reshape(n_layers, 1, D_RNN)

    return pl.pallas_call(
        _rglru_kernel,
        grid=(N_RNN_BLOCKS, t // TB_RNN),
        in_specs=[
            tile_spec,
            tile_spec,
            pl.BlockSpec((1, CONV_WIDTH, RNN_BLOCK), lambda n, i: (layer, 0, n)),
            row_spec,
            gate_w_spec,
            gate_w_spec,
            row_spec,
            row_spec,
            row_spec,
        ],
        out_specs=tile_spec,
        out_shape=jax.ShapeDtypeStruct((N_RNN_BLOCKS, t, RNN_BLOCK), BF16),
        scratch_shapes=[
            pltpu.VMEM((TB_RNN + SUBLANES, RNN_BLOCK), F32),
            pltpu.VMEM((SUBLANES, RNN_BLOCK), F32),
        ],
        compiler_params=_params(2),
        name="rglru",
    )(x_rnn, g_rnn, conv_w, rows(conv_b), w_r, w_i, rows(b_r), rows(b_i), rows(lam))


LOG2E = 1.4426950408889634
SB_EXIT_LOG2 = -104.0 * LOG2E
SB_HEADS_PER_STEP = 4


def _sb_block(qs, kvs, lower2, carries, accs, *, diagonal, valid=None):
    tq, tk = qs[0].shape[0], kvs[0][0].shape[0]
    if diagonal:
        mask = (lax.broadcasted_iota(jnp.int32, (tq, tk), 1)
                < lax.broadcasted_iota(jnp.int32, (tq, tk), 0))
    log_betas, log_keeps, split = [], [], []
    for q, (k, _) in zip(qs, kvs):
        z = _dot_nt(q, k) * (SB_HEAD_DIM ** -0.5 * LOG2E)
        l = jnp.log(1.0 + jnp.exp2(-jnp.abs(z))) * LOG2E
        log_beta = jnp.minimum(z, 0.0) - l
        log_keep = log_beta - z
        if diagonal:
            log_keep = jnp.where(mask, log_keep, 0.0)
        hi = log_keep.astype(BF16)
        lo = (log_keep - hi.astype(F32)).astype(BF16)
        log_betas.append(log_beta)
        log_keeps.append(log_keep)
        split.append(jnp.concatenate([hi, lo], axis=1))
    later_all = _dot(jnp.concatenate(split, axis=0), lower2)
    new_carries, new_accs = [], []
    for h, (_, v) in enumerate(kvs):
        x = log_betas[h] + later_all[h * tq:(h + 1) * tq]
        if carries[h] is not None:
            x = x + carries[h]
        w = jnp.exp2(x)
        if diagonal:
            w = jnp.where(mask, w, 0.0)
        if valid is not None:
            w = jnp.where(valid, w, 0.0)
        new_accs.append(accs[h] + _dot(w.astype(BF16), v))
        total = jnp.sum(log_keeps[h], axis=-1, keepdims=True)
        new_carries.append(total if carries[h] is None else carries[h] + total)
    return new_carries, new_accs


def _sb_attn_kernel(q_ref, g_ref, k_ref, v_ref, o_ref):
    i = pl.program_id(1)
    tq = q_ref.shape[0]
    heads = range(SB_HEADS_PER_STEP)
    lanes = [slice(hd * SB_HEAD_DIM, (hd + 1) * SB_HEAD_DIM) for hd in heads]
    lower = (lax.broadcasted_iota(jnp.int32, (TK_SB, TK_SB), 0)
             > lax.broadcasted_iota(jnp.int32, (TK_SB, TK_SB), 1)).astype(BF16)
    lower2 = jnp.concatenate([lower, lower], axis=0)
    qs = [q_ref[:, sl] for sl in lanes]

    def load_kv(j):
        start = pl.multiple_of(j * TK_SB, TK_SB)
        return [(k_ref[pl.ds(start, TK_SB), sl], v_ref[pl.ds(start, TK_SB), sl])
                for sl in lanes]

    def needs_more(carries):
        worst = functools.reduce(jnp.maximum, carries)
        return (jnp.max(worst) >= SB_EXIT_LOG2).astype(jnp.int32)

    carries, accs = _sb_block(qs, load_kv(i), lower2, [None for _ in heads],
                              [jnp.zeros((tq, SB_HEAD_DIM), F32) for _ in heads],
                              diagonal=True)
    carries, accs = _sb_block(qs, load_kv(jnp.maximum(i - 1, 0)), lower2, carries, accs,
                              diagonal=False, valid=i > 0)

    def cond(state):
        return jnp.logical_and(state[0] >= 0, state[1] > 0)

    def body(state):
        j = state[0]
        carries, accs = _sb_block(qs, load_kv(j), lower2, list(state[2]), list(state[3]),
                                  diagonal=False)
        return j - 1, needs_more(carries), tuple(carries), tuple(accs)

    state = lax.while_loop(cond, body,
                           (i - 2, needs_more(carries), tuple(carries), tuple(accs)))
    accs = state[3]
    for hd in heads:
        o_ref[:, lanes[hd]] = (accs[hd] * _silu(g_ref[:, lanes[hd]])).astype(o_ref.dtype)


def _sb_attn(q_sb, g_sb, kv):
    t = q_sb.shape[0]
    assert TQ_SB == TK_SB
    width = SB_HEADS_PER_STEP * SB_HEAD_DIM
    groups = N_SB_HEADS // SB_HEADS_PER_STEP
    tile_spec = pl.BlockSpec((TQ_SB, width), lambda h, i: (i, h))
    return pl.pallas_call(
        _sb_attn_kernel,
        grid=(groups, t // TQ_SB),
        in_specs=[
            tile_spec,
            tile_spec,
            pl.BlockSpec((t, width), lambda h, i: (0, h)),
            pl.BlockSpec((t, width), lambda h, i: (0, groups + h)),
        ],
        out_specs=tile_spec,
        out_shape=jax.ShapeDtypeStruct((t, D_SB), BF16),
        compiler_params=_params(2),
        name="sb_attn",
    )(q_sb, g_sb, kv, kv)


def _out_proj_kernel(ya_ref, ym_ref, h_ref, w_ref, gf_ref, o_ref, w_scr,
                     *, slab, final_norm, n_wchunks):
    i = pl.program_id(0)

    @pl.when(i < n_wchunks)
    def _():
        w_scr[i] = w_ref[0].astype(BF16)

    @pl.when(i >= n_wchunks)
    def _():
        out = h_ref[...]
        per = W_CHUNK // LANES
        for c in range(n_wchunks - 1):
            if slab:
                lhs = jnp.concatenate([ya_ref[c * per + k] for k in range(per)], axis=1)
            else:
                lhs = ya_ref[:, c * W_CHUNK:(c + 1) * W_CHUNK]
            out = out + _dot(lhs, w_scr[c])
        out = out + _dot(ym_ref[...], w_scr[n_wchunks - 1])
        if final_norm:
            out = _rms(out, gf_ref[...])
        o_ref[...] = out


def _out_proj(ya, ym, h, w_out, layer, g_final, *, slab, final_norm):
    t = h.shape[0]
    d_in = w_out.shape[1]
    d_a = d_in - D_MEM
    assert d_a % W_CHUNK == 0 and D_MEM == W_CHUNK
    n_wchunks = d_in // W_CHUNK

    def tile(i):
        return jnp.maximum(i - n_wchunks, 0)

    if slab:
        ya_spec = pl.BlockSpec((d_a // LANES, TM_OUT, LANES), lambda i: (0, tile(i), 0))
    else:
        ya_spec = pl.BlockSpec((TM_OUT, d_a), lambda i: (tile(i), 0))
    return pl.pallas_call(
        functools.partial(_out_proj_kernel, slab=slab, final_norm=final_norm,
                          n_wchunks=n_wchunks),
        grid=(n_wchunks + t // TM_OUT,),
        in_specs=[
            ya_spec,
            pl.BlockSpec((TM_OUT, D_MEM), lambda i: (tile(i), 0)),
            pl.BlockSpec((TM_OUT, D_MODEL), lambda i: (tile(i), 0)),
            pl.BlockSpec((1, W_CHUNK, D_MODEL),
                         lambda i: (layer, jnp.minimum(i, n_wchunks - 1), 0)),
            _const_spec((1, D_MODEL), (0, 0)),
        ],
        out_specs=pl.BlockSpec((TM_OUT, D_MODEL), lambda i: (tile(i), 0)),
        out_shape=jax.ShapeDtypeStruct((t, D_MODEL), F32),
        scratch_shapes=[pltpu.VMEM((n_wchunks, W_CHUNK, D_MODEL), BF16)],
        compiler_params=_params(1),
        name="out_proj",
    )(ya, ym, h, w_out, g_final.reshape(1, D_MODEL))


def _kv_proj_kernel(h_ref, g_ref, w_ref, o_ref, w_scr, *, n_wchunks):
    i = pl.program_id(0)

    @pl.when(i < n_wchunks)
    def _():
        w_scr[i] = w_ref[...].astype(BF16)

    @pl.when(i >= n_wchunks)
    def _():
        u = _rms(h_ref[...], g_ref[...]).astype(BF16)
        for c in range(n_wchunks):
            o_ref[:, c * W_CHUNK:(c + 1) * W_CHUNK] = _dot(u, w_scr[c]).astype(o_ref.dtype)


def _kv_proj(h, g, w_kv):
    t = h.shape[0]
    n_out = w_kv.shape[1]
    assert n_out % W_CHUNK == 0
    n_wchunks = n_out // W_CHUNK

    def tile(i):
        return jnp.maximum(i - n_wchunks, 0)

    return pl.pallas_call(
        functools.partial(_kv_proj_kernel, n_wchunks=n_wchunks),
        grid=(n_wchunks + t // TM_PROJ,),
        in_specs=[
            pl.BlockSpec((TM_PROJ, D_MODEL), lambda i: (tile(i), 0)),
            _const_spec((1, D_MODEL), (0, 0)),
            pl.BlockSpec((D_MODEL, W_CHUNK), lambda i: (0, jnp.minimum(i, n_wchunks - 1))),
        ],
        out_specs=pl.BlockSpec((TM_PROJ, n_out), lambda i: (tile(i), 0)),
        out_shape=jax.ShapeDtypeStruct((t, n_out), BF16),
        scratch_shapes=[pltpu.VMEM((n_wchunks, D_MODEL, W_CHUNK), BF16)],
        compiler_params=_params(1),
        name="kv_proj",
    )(h, g.reshape(1, D_MODEL), w_kv)


def kernel(x, mem, mem_norm, w_mem_kv, norm_a, w_in_a, conv_w, conv_b, w_rec_gate, b_rec_gate,
           w_in_gate, b_in_gate, lru_lambda, w_out_a, kv_norm, w_kv, norm_b, w_in_b, w_out_b,
           final_norm):
    batch = x.shape[0]
    n_a = w_in_a.shape[0]
    n_b = w_in_b.shape[0]
    norms_a = norm_a.reshape(n_a, 1, D_MODEL)
    norms_b = norm_b.reshape(n_b, 1, D_MODEL)
    outs = []
    for b in range(batch):
        memkv = _mem_kv(mem[b], mem_norm, w_mem_kv)
        h = x[b]
        for layer in range(n_a):
            x_rnn, g_rnn, y_mem = _in_proj(h, norms_a, w_in_a, memkv, layer, layer,
                                           d_a=D_RNN, a_dtype=F32, slab=True)
            y_rnn = _rglru(x_rnn, g_rnn, conv_w, conv_b, w_rec_gate, w_in_gate,
                           b_rec_gate, b_in_gate, lru_lambda, layer)
            h = _out_proj(y_rnn, y_mem, h, w_out_a, layer, final_norm,
                          slab=True, final_norm=False)
        kv = _kv_proj(h, kv_norm, w_kv)
        for j in range(n_b):
            q_sb, g_sb, y_mem = _in_proj(h, norms_b, w_in_b, memkv, j, n_a + j,
                                         d_a=D_SB, a_dtype=BF16, slab=False)
            y_sb = _sb_attn(q_sb, g_sb, kv)
            h = _out_proj(y_sb, y_mem, h, w_out_b, j, final_norm,
                          slab=False, final_norm=(j == n_b - 1))
        outs.append(h)
    return jnp.stack(outs, axis=0)
```

```python
import functools

import jax
import jax.numpy as jnp
from jax import lax
from jax.experimental import pallas as pl
from jax.experimental.pallas import tpu as pltpu

F32 = jnp.float32
BF16 = jnp.bfloat16

D_MODEL = 2048
N_MEM = 256
D_RNN = 1536
RNN_BLOCK = 128
N_RNN_BLOCKS = D_RNN // RNN_BLOCK
CONV_WIDTH = 4
LRU_C = 8.0
SB_HEAD_DIM = 128
D_SB = 1536
N_SB_HEADS = D_SB // SB_HEAD_DIM
N_MEM_HEADS = 4
MEM_HEAD_DIM = 128
D_MEM = N_MEM_HEADS * MEM_HEAD_DIM
EPS = 1e-6

SUBLANES = 8
LANES = 128

TM_PROJ = 256
TM_OUT = 512
TB_RNN = 2048
RNN_SUB = 512
TQ_SB = 256
TK_SB = 256
W_CHUNK = 512

VMEM_LIMIT = 56 * 1024 * 1024


def _rms(x, g):
    y = x * lax.rsqrt(jnp.mean(x * x, axis=-1, keepdims=True) + EPS)
    return y * g


def _sigmoid(x):
    return 0.5 * jnp.tanh(0.5 * x) + 0.5


def _silu(x):
    return x * _sigmoid(x)


def _dot(a, b):
    return jnp.dot(a, b, preferred_element_type=F32)


def _dot_nt(a, b):
    return lax.dot_general(a, b, (((1,), (1,)), ((), ())), preferred_element_type=F32)


def _const_spec(shape, index):
    return pl.BlockSpec(shape, lambda *_: index, pipeline_mode=pl.Buffered(1))


def _params(n_axes):
    return pltpu.CompilerParams(dimension_semantics=("arbitrary",) * n_axes,
                                vmem_limit_bytes=VMEM_LIMIT)


def _mem_kv_kernel(mem_ref, g_ref, w_ref, o_ref):
    u = _rms(mem_ref[...], g_ref[...]).astype(BF16)
    o_ref[0] = _dot(u, w_ref[0].astype(BF16)).astype(BF16)


def _mem_kv(mem2d, mem_norm, w_mem_kv):
    depth = w_mem_kv.shape[0]
    return pl.pallas_call(
        _mem_kv_kernel,
        grid=(depth,),
        in_specs=[
            pl.BlockSpec((N_MEM, D_MODEL), lambda l: (0, 0)),
            pl.BlockSpec((1, D_MODEL), lambda l: (0, 0)),
            pl.BlockSpec((1, D_MODEL, 2 * D_MEM), lambda l: (l, 0, 0)),
        ],
        out_specs=pl.BlockSpec((1, N_MEM, 2 * D_MEM), lambda l: (l, 0, 0)),
        out_shape=jax.ShapeDtypeStruct((depth, N_MEM, 2 * D_MEM), BF16),
        compiler_params=_params(1),
        name="mem_kv",
    )(mem2d, mem_norm.reshape(1, D_MODEL), w_mem_kv)


def _store_cols(ref, c, val, slab):
    if slab:
        per = W_CHUNK // LANES
        for k in range(per):
            ref[c * per + k] = val[:, k * LANES:(k + 1) * LANES].astype(ref.dtype)
    else:
        ref[:, c * W_CHUNK:(c + 1) * W_CHUNK] = val.astype(ref.dtype)


def _in_proj_kernel(h_ref, g_ref, w_ref, memkv_ref, a_ref, gate_ref, ymem_ref, w_scr,
                    *, d_a, slab, n_wchunks):
    i = pl.program_id(0)

    @pl.when(i < n_wchunks)
    def _():
        w_scr[i] = w_ref[0].astype(BF16)

    @pl.when(i >= n_wchunks)
    def _():
        u = _rms(h_ref[...], g_ref[0]).astype(BF16)
        na = d_a // W_CHUNK
        q_mem = _dot(u, w_scr[2 * na])
        g_mem = _dot(u, w_scr[2 * na + 1])
        for c in range(2 * na):
            dst, col = (a_ref, c) if c < na else (gate_ref, c - na)
            _store_cols(dst, col, _dot(u, w_scr[c]), slab)
            if c < N_MEM_HEADS:
                _mem_attention_head(c, q_mem, g_mem, memkv_ref, ymem_ref)


def _mem_attention_head(hd, q_mem, g_mem, memkv_ref, ymem_ref):
    sl = slice(hd * MEM_HEAD_DIM, (hd + 1) * MEM_HEAD_DIM)
    q = q_mem[:, sl].astype(BF16)
    k = memkv_ref[0, :, sl]
    v = memkv_ref[0, :, D_MEM + hd * MEM_HEAD_DIM:D_MEM + (hd + 1) * MEM_HEAD_DIM]
    s = _dot_nt(q, k) * (MEM_HEAD_DIM ** -0.5)
    e = jnp.exp(s - jnp.max(s, axis=-1, keepdims=True))
    p = e * (1.0 / jnp.sum(e, axis=-1, keepdims=True))
    o = _dot(p.astype(BF16), v)
    ymem_ref[:, sl] = (o * _silu(g_mem[:, sl])).astype(ymem_ref.dtype)


def _in_proj(h, norms, w_in, memkv, layer, mem_layer, *, d_a, a_dtype, slab):
    t = h.shape[0]
    n_out = w_in.shape[2]
    assert d_a % W_CHUNK == 0 and D_MEM == W_CHUNK and n_out == 2 * d_a + 2 * D_MEM
    n_wchunks = n_out // W_CHUNK
    n_tiles = t // TM_PROJ

    def tile(i):
        return jnp.maximum(i - n_wchunks, 0)

    if slab:
        a_shape, a_block = (d_a // LANES, t, LANES), (d_a // LANES, TM_PROJ, LANES)
        a_spec = pl.BlockSpec(a_block, lambda i: (0, tile(i), 0))
    else:
        a_shape = (t, d_a)
        a_spec = pl.BlockSpec((TM_PROJ, d_a), lambda i: (tile(i), 0))
    return pl.pallas_call(
        functools.partial(_in_proj_kernel, d_a=d_a, slab=slab, n_wchunks=n_wchunks),
        grid=(n_wchunks + n_tiles,),
        in_specs=[
            pl.BlockSpec((TM_PROJ, D_MODEL), lambda i: (tile(i), 0)),
            _const_spec((1, 1, D_MODEL), (layer, 0, 0)),
            pl.BlockSpec((1, D_MODEL, W_CHUNK),
                         lambda i: (layer, 0, jnp.minimum(i, n_wchunks - 1))),
            _const_spec((1, N_MEM, 2 * D_MEM), (mem_layer, 0, 0)),
        ],
        out_specs=[
            a_spec,
            a_spec,
            pl.BlockSpec((TM_PROJ, D_MEM), lambda i: (tile(i), 0)),
        ],
        out_shape=[
            jax.ShapeDtypeStruct(a_shape, a_dtype),
            jax.ShapeDtypeStruct(a_shape, F32),
            jax.ShapeDtypeStruct((t, D_MEM), BF16),
        ],
        scratch_shapes=[pltpu.VMEM((n_wchunks, D_MODEL, W_CHUNK), BF16)],
        compiler_params=_params(1),
        name="in_proj",
    )(h, norms, w_in, memkv)


def _rglru_kernel(x_ref, g_ref, cw_ref, cb_ref, wr_ref, wi_ref, br_ref, bi_ref, lam_ref, y_ref,
                  xpad_ref, carry_ref):
    tb = RNN_SUB
    groups = tb // SUBLANES

    @pl.when(pl.program_id(1) == 0)
    def _():
        xpad_ref[0:SUBLANES, :] = jnp.zeros((SUBLANES, RNN_BLOCK), F32)
        carry_ref[...] = jnp.zeros_like(carry_ref)

    cw = cw_ref[0]
    cb = cb_ref[0]
    w_r = wr_ref[0, 0].astype(BF16)
    w_i = wi_ref[0, 0].astype(BF16)
    nlam = -lam_ref[0]
    softplus_nlam = jnp.maximum(nlam, 0.0) + jnp.log1p(jnp.exp(-jnp.abs(nlam)))
    log_a_scale = -LRU_C * softplus_nlam

    def sub_block(sb, _):
        rows = pl.ds(pl.multiple_of(sb * tb, tb), tb)
        x = x_ref[0, rows, :]
        xpad_ref[SUBLANES:, :] = x
        xc = x * cw[CONV_WIDTH - 1:CONV_WIDTH, :] + cb
        for k in range(1, CONV_WIDTH):
            xc = xc + (xpad_ref[SUBLANES - k:SUBLANES - k + tb, :]
                       * cw[CONV_WIDTH - 1 - k:CONV_WIDTH - k, :])
        xpad_ref[0:SUBLANES, :] = x[tb - SUBLANES:, :]

        xb = xc.astype(BF16)
        r = _sigmoid(_dot(xb, w_r) + br_ref[0])
        ig = _sigmoid(_dot(xb, w_i) + bi_ref[0])
        log_a = log_a_scale * r
        a = jnp.exp(log_a)
        y2 = log_a + log_a
        e = a * a
        near_one = jnp.where(e == 1.0, -y2, (1.0 - e) * y2 / jnp.log(e))
        u = jnp.sqrt(jnp.where(e > 0.5, near_one, 1.0 - e)) * (ig * xc)

        a3 = a.reshape(groups, SUBLANES, RNN_BLOCK)
        u3 = u.reshape(groups, SUBLANES, RNN_BLOCK)
        row = lax.broadcasted_iota(jnp.int32, (groups, SUBLANES, RNN_BLOCK), 1)
        for k in (1, 2, 4):
            a_sh = pltpu.roll(a3, k, axis=1)
            u_sh = pltpu.roll(u3, k, axis=1)
            m = row >= k
            u3 = jnp.where(m, a3 * u_sh + u3, u3)
            a3 = jnp.where(m, a3 * a_sh, a3)

        carry = carry_ref[...]
        gate = g_ref[0, rows, :]
        ys = []
        for gi in range(groups):
            hg = u3[gi] + a3[gi] * carry
            ys.append(hg * _silu(gate[gi * SUBLANES:(gi + 1) * SUBLANES, :]))
            carry = jnp.broadcast_to(hg[SUBLANES - 1:SUBLANES, :], (SUBLANES, RNN_BLOCK))
        carry_ref[...] = carry
        y_ref[0, rows, :] = jnp.concatenate(ys, axis=0).astype(y_ref.dtype)
        return 0

    lax.fori_loop(0, x_ref.shape[1] // tb, sub_block, 0)


def _rglru(x_rnn, g_rnn, conv_w, conv_b, w_r, w_i, b_r, b_i, lam, layer):
    t = x_rnn.shape[1]
    n_layers = conv_w.shape[0]
    tile_spec = pl.BlockSpec((1, TB_RNN, RNN_BLOCK), lambda n, i: (n, i, 0))
    row_spec = pl.BlockSpec((1, 1, RNN_BLOCK), lambda n, i: (layer, 0, n))
    gate_w_spec = pl.BlockSpec((1, 1, RNN_BLOCK, RNN_BLOCK), lambda n, i: (layer, n, 0, 0))

    def rows(p):
        return p.reshape(n_layers, 1, D_RNN)

    return pl.pallas_call(
        _rglru_kernel,
        grid=(N_RNN_BLOCKS, t // TB_RNN),
        in_specs=[
            tile_spec,
            tile_spec,
            pl.BlockSpec((1, CONV_WIDTH, RNN_BLOCK), lambda n, i: (layer, 0, n)),
            row_spec,
            gate_w_spec,
            gate_w_spec,
            row_spec,
            row_spec,
            row_spec,
        ],
        out_specs=tile_spec,
        out_shape=jax.ShapeDtypeStruct((N_RNN_BLOCKS, t, RNN_BLOCK), BF16),
        scratch_shapes=[
            pltpu.VMEM((RNN_SUB + SUBLANES, RNN_BLOCK), F32),
            pltpu.VMEM((SUBLANES, RNN_BLOCK), F32),
        ],
        compiler_params=_params(2),
        name="rglru",
    )(x_rnn, g_rnn, conv_w, rows(conv_b), w_r, w_i, rows(b_r), rows(b_i), rows(lam))


LOG2E = 1.4426950408889634
SB_EXIT_LOG2 = -104.0 * LOG2E
SB_HEADS_PER_STEP = 4


def _sb_block(qs, kvs, lower2, carries, accs, *, diagonal, valid=None):
    tq, tk = qs[0].shape[0], kvs[0][0].shape[0]
    if diagonal:
        mask = (lax.broadcasted_iota(jnp.int32, (tq, tk), 1)
                < lax.broadcasted_iota(jnp.int32, (tq, tk), 0))
    log_betas, log_keeps, split = [], [], []
    for q, (k, _) in zip(qs, kvs):
        z = _dot_nt(q, k) * (SB_HEAD_DIM ** -0.5 * LOG2E)
        l = jnp.log(1.0 + jnp.exp2(-jnp.abs(z))) * LOG2E
        log_beta = jnp.minimum(z, 0.0) - l
        log_keep = log_beta - z
        if diagonal:
            log_keep = jnp.where(mask, log_keep, 0.0)
        hi = log_keep.astype(BF16)
        lo = (log_keep - hi.astype(F32)).astype(BF16)
        log_betas.append(log_beta)
        log_keeps.append(log_keep)
        split.append(jnp.concatenate([hi, lo], axis=1))
    later_all = _dot(jnp.concatenate(split, axis=0), lower2)
    new_carries, new_accs = [], []
    for h, (_, v) in enumerate(kvs):
        x = log_betas[h] + later_all[h * tq:(h + 1) * tq]
        if carries[h] is not None:
            x = x + carries[h]
        w = jnp.exp2(x)
        if diagonal:
            w = jnp.where(mask, w, 0.0)
        if valid is not None:
            v = jnp.where(valid, v, jnp.zeros_like(v))
        new_accs.append(accs[h] + _dot(w.astype(BF16), v))
        total = jnp.sum(log_keeps[h], axis=-1, keepdims=True)
        new_carries.append(total if carries[h] is None else carries[h] + total)
    return new_carries, new_accs


def _sb_attn_kernel(q_ref, g_ref, k_ref, v_ref, o_ref):
    i = pl.program_id(1)
    tq = q_ref.shape[0]
    heads = range(SB_HEADS_PER_STEP)
    lanes = [slice(hd * SB_HEAD_DIM, (hd + 1) * SB_HEAD_DIM) for hd in heads]
    lower = (lax.broadcasted_iota(jnp.int32, (TK_SB, TK_SB), 0)
             > lax.broadcasted_iota(jnp.int32, (TK_SB, TK_SB), 1)).astype(BF16)
    lower2 = jnp.concatenate([lower, lower], axis=0)
    qs = [q_ref[:, sl] for sl in lanes]

    def load_kv(j):
        start = pl.multiple_of(j * TK_SB, TK_SB)
        return [(k_ref[pl.ds(start, TK_SB), sl], v_ref[pl.ds(start, TK_SB), sl])
                for sl in lanes]

    def needs_more(carries):
        worst = functools.reduce(jnp.maximum, carries)
        return (jnp.max(worst) >= SB_EXIT_LOG2).astype(jnp.int32)

    carries, accs = _sb_block(qs, load_kv(i), lower2, [None for _ in heads],
                              [jnp.zeros((tq, SB_HEAD_DIM), F32) for _ in heads],
                              diagonal=True)
    carries, accs = _sb_block(qs, load_kv(jnp.maximum(i - 1, 0)), lower2, carries, accs,
                              diagonal=False, valid=i > 0)

    def cond(state):
        return jnp.logical_and(state[0] >= 0, state[1] > 0)

    def body(state):
        j = state[0]
        carries, accs = _sb_block(qs, load_kv(j), lower2, list(state[2]), list(state[3]),
                                  diagonal=False)
        return j - 1, needs_more(carries), tuple(carries), tuple(accs)

    state = lax.while_loop(cond, body,
                           (i - 2, needs_more(carries), tuple(carries), tuple(accs)))
    accs = state[3]
    for hd in heads:
        o_ref[:, lanes[hd]] = (accs[hd] * _silu(g_ref[:, lanes[hd]])).astype(o_ref.dtype)


def _sb_attn(q_sb, g_sb, kv):
    t = q_sb.shape[0]
    assert TQ_SB == TK_SB
    width = SB_HEADS_PER_STEP * SB_HEAD_DIM
    groups = N_SB_HEADS // SB_HEADS_PER_STEP
    tile_spec = pl.BlockSpec((TQ_SB, width), lambda h, i: (i, h))
    return pl.pallas_call(
        _sb_attn_kernel,
        grid=(groups, t // TQ_SB),
        in_specs=[
            tile_spec,
            tile_spec,
            pl.BlockSpec((t, width), lambda h, i: (0, h)),
            pl.BlockSpec((t, width), lambda h, i: (0, groups + h)),
        ],
        out_specs=tile_spec,
        out_shape=jax.ShapeDtypeStruct((t, D_SB), BF16),
        compiler_params=_params(2),
        name="sb_attn",
    )(q_sb, g_sb, kv, kv)


def _out_proj_kernel(ya_ref, ym_ref, h_ref, w_ref, gf_ref, o_ref, w_scr,
                     *, slab, final_norm, n_wchunks):
    i = pl.program_id(0)

    @pl.when(i < n_wchunks)
    def _():
        w_scr[i] = w_ref[0].astype(BF16)

    @pl.when(i >= n_wchunks)
    def _():
        out = h_ref[...]
        per = W_CHUNK // LANES
        for c in range(n_wchunks - 1):
            if slab:
                lhs = jnp.concatenate([ya_ref[c * per + k] for k in range(per)], axis=1)
            else:
                lhs = ya_ref[:, c * W_CHUNK:(c + 1) * W_CHUNK]
            out = out + _dot(lhs, w_scr[c])
        out = out + _dot(ym_ref[...], w_scr[n_wchunks - 1])
        if final_norm:
            out = _rms(out, gf_ref[...])
        o_ref[...] = out


def _out_proj(ya, ym, h, w_out, layer, g_final, *, slab, final_norm):
    t = h.shape[0]
    d_in = w_out.shape[1]
    d_a = d_in - D_MEM
    assert d_a % W_CHUNK == 0 and D_MEM == W_CHUNK
    n_wchunks = d_in // W_CHUNK

    def tile(i):
        return jnp.maximum(i - n_wchunks, 0)

    if slab:
        ya_spec = pl.BlockSpec((d_a // LANES, TM_OUT, LANES), lambda i: (0, tile(i), 0))
    else:
        ya_spec = pl.BlockSpec((TM_OUT, d_a), lambda i: (tile(i), 0))
    return pl.pallas_call(
        functools.partial(_out_proj_kernel, slab=slab, final_norm=final_norm,
                          n_wchunks=n_wchunks),
        grid=(n_wchunks + t // TM_OUT,),
        in_specs=[
            ya_spec,
            pl.BlockSpec((TM_OUT, D_MEM), lambda i: (tile(i), 0)),
            pl.BlockSpec((TM_OUT, D_MODEL), lambda i: (tile(i), 0)),
            pl.BlockSpec((1, W_CHUNK, D_MODEL),
                         lambda i: (layer, jnp.minimum(i, n_wchunks - 1), 0)),
            _const_spec((1, D_MODEL), (0, 0)),
        ],
        out_specs=pl.BlockSpec((TM_OUT, D_MODEL), lambda i: (tile(i), 0)),
        out_shape=jax.ShapeDtypeStruct((t, D_MODEL), F32),
        scratch_shapes=[pltpu.VMEM((n_wchunks, W_CHUNK, D_MODEL), BF16)],
        compiler_params=_params(1),
        name="out_proj",
    )(ya, ym, h, w_out, g_final.reshape(1, D_MODEL))


def _kv_proj_kernel(h_ref, g_ref, w_ref, o_ref, w_scr, *, n_wchunks):
    i = pl.program_id(0)

    @pl.when(i < n_wchunks)
    def _():
        w_scr[i] = w_ref[...].astype(BF16)

    @pl.when(i >= n_wchunks)
    def _():
        u = _rms(h_ref[...], g_ref[...]).astype(BF16)
        for c in range(n_wchunks):
            o_ref[:, c * W_CHUNK:(c + 1) * W_CHUNK] = _dot(u, w_scr[c]).astype(o_ref.dtype)


def _kv_proj(h, g, w_kv):
    t = h.shape[0]
    n_out = w_kv.shape[1]
    assert n_out % W_CHUNK == 0
    n_wchunks = n_out // W_CHUNK

    def tile(i):
        return jnp.maximum(i - n_wchunks, 0)

    return pl.pallas_call(
        functools.partial(_kv_proj_kernel, n_wchunks=n_wchunks),
        grid=(n_wchunks + t // TM_OUT,),
        in_specs=[
            pl.BlockSpec((TM_OUT, D_MODEL), lambda i: (tile(i), 0)),
            _const_spec((1, D_MODEL), (0, 0)),
            pl.BlockSpec((D_MODEL, W_CHUNK), lambda i: (0, jnp.minimum(i, n_wchunks - 1))),
        ],
        out_specs=pl.BlockSpec((TM_OUT, n_out), lambda i: (tile(i), 0)),
        out_shape=jax.ShapeDtypeStruct((t, n_out), BF16),
        scratch_shapes=[pltpu.VMEM((n_wchunks, D_MODEL, W_CHUNK), BF16)],
        compiler_params=_params(1),
        name="kv_proj",
    )(h, g.reshape(1, D_MODEL), w_kv)


def kernel(x, mem, mem_norm, w_mem_kv, norm_a, w_in_a, conv_w, conv_b, w_rec_gate, b_rec_gate,
           w_in_gate, b_in_gate, lru_lambda, w_out_a, kv_norm, w_kv, norm_b, w_in_b, w_out_b,
           final_norm):
    batch = x.shape[0]
    n_a = w_in_a.shape[0]
    n_b = w_in_b.shape[0]
    norms_a = norm_a.reshape(n_a, 1, D_MODEL)
    norms_b = norm_b.reshape(n_b, 1, D_MODEL)
    outs = []
    for b in range(batch):
        memkv = _mem_kv(mem[b], mem_norm, w_mem_kv)
        h = x[b]
        for layer in range(n_a):
            x_rnn, g_rnn, y_mem = _in_proj(h, norms_a, w_in_a, memkv, layer, layer,
                                           d_a=D_RNN, a_dtype=F32, slab=True)
            y_rnn = _rglru(x_rnn, g_rnn, conv_w, conv_b, w_rec_gate, w_in_gate,
                           b_rec_gate, b_in_gate, lru_lambda, layer)
            h = _out_proj(y_rnn, y_mem, h, w_out_a, layer, final_norm,
                          slab=True, final_norm=False)
        kv = _kv_proj(h, kv_norm, w_kv)
        for j in range(n_b):
            q_sb, g_sb, y_mem = _in_proj(h, norms_b, w_in_b, memkv, j, n_a + j,
                                         d_a=D_SB, a_dtype=BF16, slab=False)
            y_sb = _sb_attn(q_sb, g_sb, kv)
            h = _out_proj(y_sb, y_mem, h, w_out_b, j, final_norm,
                          slab=False, final_norm=(j == n_b - 1))
        outs.append(h)
    return jnp.stack(outs, axis=0)
```

```python
import functools

import jax
import jax.numpy as jnp
from jax import lax
from jax.experimental import pallas as pl
from jax.experimental.pallas import tpu as pltpu

F32 = jnp.float32
BF16 = jnp.bfloat16

D_MODEL = 2048
N_MEM = 256
D_RNN = 1536
RNN_BLOCK = 128
N_RNN_BLOCKS = D_RNN // RNN_BLOCK
CONV_WIDTH = 4
LRU_C = 8.0
SB_HEAD_DIM = 128
D_SB = 1536
N_SB_HEADS = D_SB // SB_HEAD_DIM
N_MEM_HEADS = 4
MEM_HEAD_DIM = 128
D_MEM = N_MEM_HEADS * MEM_HEAD_DIM
EPS = 1e-6

SUBLANES = 8
LANES = 128

TM_PROJ = 512
PROJ_ROWS = 256
TM_OUT = 512
TB_RNN = 2048
RNN_SUB = 512
TQ_SB = 256
TK_SB = 256
W_CHUNK = 512

VMEM_LIMIT = 56 * 1024 * 1024


def _rms(x, g):
    y = x * lax.rsqrt(jnp.mean(x * x, axis=-1, keepdims=True) + EPS)
    return y * g


def _sigmoid(x):
    return 0.5 * jnp.tanh(0.5 * x) + 0.5


def _silu(x):
    half = 0.5 * x
    return half * (jnp.tanh(half) + 1.0)


def _dot(a, b):
    return jnp.dot(a, b, preferred_element_type=F32)


def _dot_nt(a, b):
    return lax.dot_general(a, b, (((1,), (1,)), ((), ())), preferred_element_type=F32)


def _const_spec(shape, index):
    return pl.BlockSpec(shape, lambda *_: index, pipeline_mode=pl.Buffered(1))


def _params(n_axes):
    return pltpu.CompilerParams(dimension_semantics=("arbitrary",) * n_axes,
                                vmem_limit_bytes=VMEM_LIMIT)


def _mem_kv_kernel(mem_ref, g_ref, w_ref, o_ref):
    u = _rms(mem_ref[...], g_ref[...]).astype(BF16)
    o_ref[0] = _dot(u, w_ref[0].astype(BF16)).astype(BF16)


def _mem_kv(mem2d, mem_norm, w_mem_kv):
    depth = w_mem_kv.shape[0]
    return pl.pallas_call(
        _mem_kv_kernel,
        grid=(depth,),
        in_specs=[
            pl.BlockSpec((N_MEM, D_MODEL), lambda l: (0, 0)),
            pl.BlockSpec((1, D_MODEL), lambda l: (0, 0)),
            pl.BlockSpec((1, D_MODEL, 2 * D_MEM), lambda l: (l, 0, 0)),
        ],
        out_specs=pl.BlockSpec((1, N_MEM, 2 * D_MEM), lambda l: (l, 0, 0)),
        out_shape=jax.ShapeDtypeStruct((depth, N_MEM, 2 * D_MEM), BF16),
        compiler_params=_params(1),
        name="mem_kv",
    )(mem2d, mem_norm.reshape(1, D_MODEL), w_mem_kv)


def _store_cols(ref, rows, c, val, slab):
    if slab:
        per = W_CHUNK // LANES
        for k in range(per):
            ref[c * per + k, rows, :] = val[:, k * LANES:(k + 1) * LANES].astype(ref.dtype)
    else:
        ref[rows, c * W_CHUNK:(c + 1) * W_CHUNK] = val.astype(ref.dtype)


def _in_proj_kernel(h_ref, g_ref, w_ref, memkv_ref, a_ref, gate_ref, ymem_ref, w_scr,
                    *, d_a, slab, n_wchunks):
    i = pl.program_id(0)

    @pl.when(i < n_wchunks)
    def _():
        w_scr[i] = w_ref[0].astype(BF16)

    @pl.when(i >= n_wchunks)
    def _():
        na = d_a // W_CHUNK
        for r0 in range(0, h_ref.shape[0], PROJ_ROWS):
            rows = slice(r0, r0 + PROJ_ROWS)
            u = _rms(h_ref[rows, :], g_ref[0]).astype(BF16)
            q_mem = _dot(u, w_scr[2 * na])
            g_mem = _dot(u, w_scr[2 * na + 1])
            for c in range(2 * na):
                dst, col = (a_ref, c) if c < na else (gate_ref, c - na)
                _store_cols(dst, rows, col, _dot(u, w_scr[c]), slab)
                if c < N_MEM_HEADS:
                    _mem_attention_head(c, rows, q_mem, g_mem, memkv_ref, ymem_ref)


def _mem_attention_head(hd, rows, q_mem, g_mem, memkv_ref, ymem_ref):
    sl = slice(hd * MEM_HEAD_DIM, (hd + 1) * MEM_HEAD_DIM)
    q = q_mem[:, sl].astype(BF16)
    k = memkv_ref[0, :, sl]
    v = memkv_ref[0, :, D_MEM + hd * MEM_HEAD_DIM:D_MEM + (hd + 1) * MEM_HEAD_DIM]
    s = _dot_nt(q, k) * (MEM_HEAD_DIM ** -0.5)
    e = jnp.exp(s - jnp.max(s, axis=-1, keepdims=True))
    p = e * (1.0 / jnp.sum(e, axis=-1, keepdims=True))
    o = _dot(p.astype(BF16), v)
    ymem_ref[rows, sl] = (o * _silu(g_mem[:, sl])).astype(ymem_ref.dtype)


def _in_proj(h, norms, w_in, memkv, layer, mem_layer, *, d_a, a_dtype, slab):
    t = h.shape[0]
    n_out = w_in.shape[2]
    assert d_a % W_CHUNK == 0 and D_MEM == W_CHUNK and n_out == 2 * d_a + 2 * D_MEM
    n_wchunks = n_out // W_CHUNK
    n_tiles = t // TM_PROJ

    def tile(i):
        return jnp.maximum(i - n_wchunks, 0)

    if slab:
        a_shape, a_block = (d_a // LANES, t, LANES), (d_a // LANES, TM_PROJ, LANES)
        a_spec = pl.BlockSpec(a_block, lambda i: (0, tile(i), 0))
    else:
        a_shape = (t, d_a)
        a_spec = pl.BlockSpec((TM_PROJ, d_a), lambda i: (tile(i), 0))
    return pl.pallas_call(
        functools.partial(_in_proj_kernel, d_a=d_a, slab=slab, n_wchunks=n_wchunks),
        grid=(n_wchunks + n_tiles,),
        in_specs=[
            pl.BlockSpec((TM_PROJ, D_MODEL), lambda i: (tile(i), 0)),
            _const_spec((1, 1, D_MODEL), (layer, 0, 0)),
            pl.BlockSpec((1, D_MODEL, W_CHUNK),
                         lambda i: (layer, 0, jnp.minimum(i, n_wchunks - 1))),
            _const_spec((1, N_MEM, 2 * D_MEM), (mem_layer, 0, 0)),
        ],
        out_specs=[
            a_spec,
            a_spec,
            pl.BlockSpec((TM_PROJ, D_MEM), lambda i: (tile(i), 0)),
        ],
        out_shape=[
            jax.ShapeDtypeStruct(a_shape, a_dtype),
            jax.ShapeDtypeStruct(a_shape, F32),
            jax.ShapeDtypeStruct((t, D_MEM), BF16),
        ],
        scratch_shapes=[pltpu.VMEM((n_wchunks, D_MODEL, W_CHUNK), BF16)],
        compiler_params=_params(1),
        name="in_proj",
    )(h, norms, w_in, memkv)


def _rglru_kernel(x_ref, g_ref, cw_ref, cb_ref, wr_ref, wi_ref, br_ref, bi_ref, lam_ref, y_ref,
                  xpad_ref, carry_ref):
    tb = RNN_SUB
    groups = tb // SUBLANES

    @pl.when(pl.program_id(1) == 0)
    def _():
        xpad_ref[0:SUBLANES, :] = jnp.zeros((SUBLANES, RNN_BLOCK), F32)
        carry_ref[...] = jnp.zeros_like(carry_ref)

    cw = cw_ref[0]
    cb = cb_ref[0]
    w_r = wr_ref[0, 0].astype(BF16)
    w_i = wi_ref[0, 0].astype(BF16)
    nlam = -lam_ref[0]
    softplus_nlam = jnp.maximum(nlam, 0.0) + jnp.log1p(jnp.exp(-jnp.abs(nlam)))
    half_scale = 0.5 * (-LRU_C * softplus_nlam)

    def sub_block(sb, _):
        rows = pl.ds(pl.multiple_of(sb * tb, tb), tb)
        x = x_ref[0, rows, :]
        xpad_ref[SUBLANES:, :] = x
        xc = x * cw[CONV_WIDTH - 1:CONV_WIDTH, :] + cb
        for k in range(1, CONV_WIDTH):
            xc = xc + (xpad_ref[SUBLANES - k:SUBLANES - k + tb, :]
                       * cw[CONV_WIDTH - 1 - k:CONV_WIDTH - k, :])
        xpad_ref[0:SUBLANES, :] = x[tb - SUBLANES:, :]

        xb = xc.astype(BF16)
        ig = _sigmoid(_dot(xb, w_i) + bi_ref[0])
        log_a = half_scale * jnp.tanh(0.5 * (_dot(xb, w_r) + br_ref[0])) + half_scale
        a = jnp.exp(log_a)
        y2 = log_a + log_a
        e = a * a
        near_one = jnp.where(e == 1.0, -y2, (1.0 - e) * y2 / jnp.log(e))
        u = jnp.sqrt(jnp.where(e > 0.5, near_one, 1.0 - e)) * (ig * xc)

        a3 = a.reshape(groups, SUBLANES, RNN_BLOCK)
        u3 = u.reshape(groups, SUBLANES, RNN_BLOCK)
        row = lax.broadcasted_iota(jnp.int32, (groups, SUBLANES, RNN_BLOCK), 1)
        for k in (1, 2, 4):
            a_sh = pltpu.roll(a3, k, axis=1)
            u_sh = pltpu.roll(u3, k, axis=1)
            m = row >= k
            u3 = jnp.where(m, a3 * u_sh + u3, u3)
            a3 = jnp.where(m, a3 * a_sh, a3)

        carry = carry_ref[...]
        gate = g_ref[0, rows, :]
        ys = []
        for gi in range(groups):
            hg = u3[gi] + a3[gi] * carry
            ys.append(hg * _silu(gate[gi * SUBLANES:(gi + 1) * SUBLANES, :]))
            carry = jnp.broadcast_to(hg[SUBLANES - 1:SUBLANES, :], (SUBLANES, RNN_BLOCK))
        carry_ref[...] = carry
        y_ref[0, rows, :] = jnp.concatenate(ys, axis=0).astype(y_ref.dtype)
        return 0

    lax.fori_loop(0, x_ref.shape[1] // tb, sub_block, 0)


def _rglru(x_rnn, g_rnn, conv_w, conv_b, w_r, w_i, b_r, b_i, lam, layer):
    t = x_rnn.shape[1]
    n_layers = conv_w.shape[0]
    tile_spec = pl.BlockSpec((1, TB_RNN, RNN_BLOCK), lambda n, i: (n, i, 0))
    row_spec = pl.BlockSpec((1, 1, RNN_BLOCK), lambda n, i: (layer, 0, n))
    gate_w_spec = pl.BlockSpec((1, 1, RNN_BLOCK, RNN_BLOCK), lambda n, i: (layer, n, 0, 0))

    def rows(p):
        return p.reshape(n_layers, 1, D_RNN)

    return pl.pallas_call(
        _rglru_kernel,
        grid=(N_RNN_BLOCKS, t // TB_RNN),
        in_specs=[
            tile_spec,
            tile_spec,
            pl.BlockSpec((1, CONV_WIDTH, RNN_BLOCK), lambda n, i: (layer, 0, n)),
            row_spec,
            gate_w_spec,
            gate_w_spec,
            row_spec,
            row_spec,
            row_spec,
        ],
        out_specs=tile_spec,
        out_shape=jax.ShapeDtypeStruct((N_RNN_BLOCKS, t, RNN_BLOCK), BF16),
        scratch_shapes=[
            pltpu.VMEM((RNN_SUB + SUBLANES, RNN_BLOCK), F32),
            pltpu.VMEM((SUBLANES, RNN_BLOCK), F32),
        ],
        compiler_params=_params(2),
        name="rglru",
    )(x_rnn, g_rnn, conv_w, rows(conv_b), w_r, w_i, rows(b_r), rows(b_i), rows(lam))


LOG2E = 1.4426950408889634
SB_EXIT_LOG2 = -104.0 * LOG2E
SB_HEADS_PER_STEP = 4


def _sb_block(qs, kvs, lower2, carries, accs, *, diagonal, valid=None):
    tq, tk = qs[0].shape[0], kvs[0][0].shape[0]
    if diagonal:
        mask = (lax.broadcasted_iota(jnp.int32, (tq, tk), 1)
                < lax.broadcasted_iota(jnp.int32, (tq, tk), 0))
    log_betas, log_keeps, split = [], [], []
    for q, (k, _) in zip(qs, kvs):
        z = _dot_nt(q, k) * (SB_HEAD_DIM ** -0.5 * LOG2E)
        l = jnp.log(1.0 + jnp.exp2(-jnp.abs(z))) * LOG2E
        log_beta = jnp.minimum(z, 0.0) - l
        log_keep = log_beta - z
        if diagonal:
            log_keep = jnp.where(mask, log_keep, 0.0)
        hi = log_keep.astype(BF16)
        lo = (log_keep - hi.astype(F32)).astype(BF16)
        log_betas.append(log_beta)
        log_keeps.append(log_keep)
        split.append(jnp.concatenate([hi, lo], axis=1))
    later_all = _dot(jnp.concatenate(split, axis=0), lower2)
    new_carries, new_accs = [], []
    for h, (_, v) in enumerate(kvs):
        x = log_betas[h] + later_all[h * tq:(h + 1) * tq]
        if carries[h] is not None:
            x = x + carries[h]
        w = jnp.exp2(x)
        if diagonal:
            w = jnp.where(mask, w, 0.0)
        if valid is not None:
            v = jnp.where(valid, v, jnp.zeros_like(v))
        new_accs.append(accs[h] + _dot(w.astype(BF16), v))
        total = jnp.sum(log_keeps[h], axis=-1, keepdims=True)
        new_carries.append(total if carries[h] is None else carries[h] + total)
    return new_carries, new_accs


def _sb_attn_kernel(q_ref, g_ref, k_ref, v_ref, o_ref):
    i = pl.program_id(1)
    tq = q_ref.shape[0]
    heads = range(SB_HEADS_PER_STEP)
    lanes = [slice(hd * SB_HEAD_DIM, (hd + 1) * SB_HEAD_DIM) for hd in heads]
    lower = (lax.broadcasted_iota(jnp.int32, (TK_SB, TK_SB), 0)
             > lax.broadcasted_iota(jnp.int32, (TK_SB, TK_SB), 1)).astype(BF16)
    lower2 = jnp.concatenate([lower, lower], axis=0)
    qs = [q_ref[:, sl] for sl in lanes]

    def load_kv(j):
        start = pl.multiple_of(j * TK_SB, TK_SB)
        return [(k_ref[pl.ds(start, TK_SB), sl], v_ref[pl.ds(start, TK_SB), sl])
                for sl in lanes]

    def needs_more(carries):
        worst = functools.reduce(jnp.maximum, carries)
        return (jnp.max(worst) >= SB_EXIT_LOG2).astype(jnp.int32)

    carries, accs = _sb_block(qs, load_kv(i), lower2, [None for _ in heads],
                              [jnp.zeros((tq, SB_HEAD_DIM), F32) for _ in heads],
                              diagonal=True)
    carries, accs = _sb_block(qs, load_kv(jnp.maximum(i - 1, 0)), lower2, carries, accs,
                              diagonal=False, valid=i > 0)

    def cond(state):
        return jnp.logical_and(state[0] >= 0, state[1] > 0)

    def body(state):
        j = state[0]
        carries, accs = _sb_block(qs, load_kv(j), lower2, list(state[2]), list(state[3]),
                                  diagonal=False)
        return j - 1, needs_more(carries), tuple(carries), tuple(accs)

    state = lax.while_loop(cond, body,
                           (i - 2, needs_more(carries), tuple(carries), tuple(accs)))
    accs = state[3]
    for hd in heads:
        o_ref[:, lanes[hd]] = (accs[hd] * _silu(g_ref[:, lanes[hd]])).astype(o_ref.dtype)


def _sb_attn(q_sb, g_sb, kv):
    t = q_sb.shape[0]
    assert TQ_SB == TK_SB
    width = SB_HEADS_PER_STEP * SB_HEAD_DIM
    groups = N_SB_HEADS // SB_HEADS_PER_STEP
    tile_spec = pl.BlockSpec((TQ_SB, width), lambda h, i: (i, h))
    return pl.pallas_call(
        _sb_attn_kernel,
        grid=(groups, t // TQ_SB),
        in_specs=[
            tile_spec,
            tile_spec,
            pl.BlockSpec((t, width), lambda h, i: (0, h)),
            pl.BlockSpec((t, width), lambda h, i: (0, groups + h)),
        ],
        out_specs=tile_spec,
        out_shape=jax.ShapeDtypeStruct((t, D_SB), BF16),
        compiler_params=_params(2),
        name="sb_attn",
    )(q_sb, g_sb, kv, kv)


def _out_proj_kernel(ya_ref, ym_ref, h_ref, w_ref, gf_ref, o_ref, w_scr,
                     *, slab, final_norm, n_wchunks):
    i = pl.program_id(0)

    @pl.when(i < n_wchunks)
    def _():
        w_scr[i] = w_ref[0].astype(BF16)

    @pl.when(i >= n_wchunks)
    def _():
        out = h_ref[...]
        per = W_CHUNK // LANES
        for c in range(n_wchunks - 1):
            if slab:
                lhs = jnp.concatenate([ya_ref[c * per + k] for k in range(per)], axis=1)
            else:
                lhs = ya_ref[:, c * W_CHUNK:(c + 1) * W_CHUNK]
            out = out + _dot(lhs, w_scr[c])
        out = out + _dot(ym_ref[...], w_scr[n_wchunks - 1])
        if final_norm:
            out = _rms(out, gf_ref[...])
        o_ref[...] = out


def _out_proj(ya, ym, h, w_out, layer, g_final, *, slab, final_norm):
    t = h.shape[0]
    d_in = w_out.shape[1]
    d_a = d_in - D_MEM
    assert d_a % W_CHUNK == 0 and D_MEM == W_CHUNK
    n_wchunks = d_in // W_CHUNK

    def tile(i):
        return jnp.maximum(i - n_wchunks, 0)

    if slab:
        ya_spec = pl.BlockSpec((d_a // LANES, TM_OUT, LANES), lambda i: (0, tile(i), 0))
    else:
        ya_spec = pl.BlockSpec((TM_OUT, d_a), lambda i: (tile(i), 0))
    return pl.pallas_call(
        functools.partial(_out_proj_kernel, slab=slab, final_norm=final_norm,
                          n_wchunks=n_wchunks),
        grid=(n_wchunks + t // TM_OUT,),
        in_specs=[
            ya_spec,
            pl.BlockSpec((TM_OUT, D_MEM), lambda i: (tile(i), 0)),
            pl.BlockSpec((TM_OUT, D_MODEL), lambda i: (tile(i), 0)),
            pl.BlockSpec((1, W_CHUNK, D_MODEL),
                         lambda i: (layer, jnp.minimum(i, n_wchunks - 1), 0)),
            _const_spec((1, D_MODEL), (0, 0)),
        ],
        out_specs=pl.BlockSpec((TM_OUT, D_MODEL), lambda i: (tile(i), 0)),
        out_shape=jax.ShapeDtypeStruct((t, D_MODEL), F32),
        scratch_shapes=[pltpu.VMEM((n_wchunks, W_CHUNK, D_MODEL), BF16)],
        compiler_params=_params(1),
        name="out_proj",
    )(ya, ym, h, w_out, g_final.reshape(1, D_MODEL))


def _kv_proj_kernel(h_ref, g_ref, w_ref, o_ref, w_scr, *, n_wchunks):
    i = pl.program_id(0)

    @pl.when(i < n_wchunks)
    def _():
        w_scr[i] = w_ref[...].astype(BF16)

    @pl.when(i >= n_wchunks)
    def _():
        for r0 in range(0, h_ref.shape[0], PROJ_ROWS):
            rows = slice(r0, r0 + PROJ_ROWS)
            u = _rms(h_ref[rows, :], g_ref[...]).astype(BF16)
            for c in range(n_wchunks):
                o_ref[rows, c * W_CHUNK:(c + 1) * W_CHUNK] = _dot(u, w_scr[c]).astype(o_ref.dtype)


def _kv_proj(h, g, w_kv):
    t = h.shape[0]
    n_out = w_kv.shape[1]
    assert n_out % W_CHUNK == 0
    n_wchunks = n_out // W_CHUNK

    def tile(i):
        return jnp.maximum(i - n_wchunks, 0)

    return pl.pallas_call(
        functools.partial(_kv_proj_kernel, n_wchunks=n_wchunks),
        grid=(n_wchunks + t // TM_OUT,),
        in_specs=[
            pl.BlockSpec((TM_OUT, D_MODEL), lambda i: (tile(i), 0)),
            _const_spec((1, D_MODEL), (0, 0)),
            pl.BlockSpec((D_MODEL, W_CHUNK), lambda i: (0, jnp.minimum(i, n_wchunks - 1))),
        ],
        out_specs=pl.BlockSpec((TM_OUT, n_out), lambda i: (tile(i), 0)),
        out_shape=jax.ShapeDtypeStruct((t, n_out), BF16),
        scratch_shapes=[pltpu.VMEM((n_wchunks, D_MODEL, W_CHUNK), BF16)],
        compiler_params=_params(1),
        name="kv_proj",
    )(h, g.reshape(1, D_MODEL), w_kv)


def kernel(x, mem, mem_norm, w_mem_kv, norm_a, w_in_a, conv_w, conv_b, w_rec_gate, b_rec_gate,
           w_in_gate, b_in_gate, lru_lambda, w_out_a, kv_norm, w_kv, norm_b, w_in_b, w_out_b,
           final_norm):
    batch = x.shape[0]
    n_a = w_in_a.shape[0]
    n_b = w_in_b.shape[0]
    norms_a = norm_a.reshape(n_a, 1, D_MODEL)
    norms_b = norm_b.reshape(n_b, 1, D_MODEL)
    outs = []
    for b in range(batch):
        memkv = _mem_kv(mem[b], mem_norm, w_mem_kv)
        h = x[b]
        for layer in range(n_a):
            x_rnn, g_rnn, y_mem = _in_proj(h, norms_a, w_in_a, memkv, layer, layer,
                                           d_a=D_RNN, a_dtype=F32, slab=True)
            y_rnn = _rglru(x_rnn, g_rnn, conv_w, conv_b, w_rec_gate, w_in_gate,
                           b_rec_gate, b_in_gate, lru_lambda, layer)
            h = _out_proj(y_rnn, y_mem, h, w_out_a, layer, final_norm,
                          slab=True, final_norm=False)
        kv = _kv_proj(h, kv_norm, w_kv)
        for j in range(n_b):
            q_sb, g_sb, y_mem = _in_proj(h, norms_b, w_in_b, memkv, j, n_a + j,
                                         d_a=D_SB, a_dtype=BF16, slab=False)
            y_sb = _sb_attn(q_sb, g_sb, kv)
            h = _out_proj(y_sb, y_mem, h, w_out_b, j, final_norm,
                          slab=False, final_norm=(j == n_b - 1))
        outs.append(h)
    return jnp.stack(outs, axis=0)
```

```python
import functools

import jax
import jax.numpy as jnp
from jax import lax
from jax.experimental import pallas as pl
from jax.experimental.pallas import tpu as pltpu

F32 = jnp.float32
BF16 = jnp.bfloat16

D_MODEL = 2048
N_MEM = 256
D_RNN = 1536
RNN_BLOCK = 128
N_RNN_BLOCKS = D_RNN // RNN_BLOCK
CONV_WIDTH = 4
LRU_C = 8.0
SB_HEAD_DIM = 128
D_SB = 1536
N_SB_HEADS = D_SB // SB_HEAD_DIM
N_MEM_HEADS = 4
MEM_HEAD_DIM = 128
D_MEM = N_MEM_HEADS * MEM_HEAD_DIM
EPS = 1e-6

SUBLANES = 8
LANES = 128

TM_PROJ = 512
PROJ_ROWS = 256
TM_OUT = 512
TM_RNN = 256
TQ_SB = 256
TK_SB = 256
W_CHUNK = 512

VMEM_LIMIT = 56 * 1024 * 1024


def _rms(x, g):
    y = x * lax.rsqrt(jnp.mean(x * x, axis=-1, keepdims=True) + EPS)
    return y * g


def _sigmoid(x):
    return 0.5 * jnp.tanh(0.5 * x) + 0.5


def _silu(x):
    half = 0.5 * x
    return half * (jnp.tanh(half) + 1.0)


def _dot(a, b):
    return jnp.dot(a, b, preferred_element_type=F32)


def _dot_nt(a, b):
    return lax.dot_general(a, b, (((1,), (1,)), ((), ())), preferred_element_type=F32)


def _const_spec(shape, index):
    return pl.BlockSpec(shape, lambda *_: index, pipeline_mode=pl.Buffered(1))


def _params(n_axes):
    return pltpu.CompilerParams(dimension_semantics=("arbitrary",) * n_axes,
                                vmem_limit_bytes=VMEM_LIMIT)


def _mem_kv_kernel(mem_ref, g_ref, w_ref, o_ref):
    u = _rms(mem_ref[...], g_ref[...]).astype(BF16)
    o_ref[0] = _dot(u, w_ref[0].astype(BF16)).astype(BF16)


def _mem_kv(mem2d, mem_norm, w_mem_kv):
    depth = w_mem_kv.shape[0]
    return pl.pallas_call(
        _mem_kv_kernel,
        grid=(depth,),
        in_specs=[
            pl.BlockSpec((N_MEM, D_MODEL), lambda l: (0, 0)),
            pl.BlockSpec((1, D_MODEL), lambda l: (0, 0)),
            pl.BlockSpec((1, D_MODEL, 2 * D_MEM), lambda l: (l, 0, 0)),
        ],
        out_specs=pl.BlockSpec((1, N_MEM, 2 * D_MEM), lambda l: (l, 0, 0)),
        out_shape=jax.ShapeDtypeStruct((depth, N_MEM, 2 * D_MEM), BF16),
        compiler_params=_params(1),
        name="mem_kv",
    )(mem2d, mem_norm.reshape(1, D_MODEL), w_mem_kv)


def _store_cols(ref, rows, c, val, slab):
    if slab:
        per = W_CHUNK // LANES
        for k in range(per):
            ref[c * per + k, rows, :] = val[:, k * LANES:(k + 1) * LANES].astype(ref.dtype)
    else:
        ref[rows, c * W_CHUNK:(c + 1) * W_CHUNK] = val.astype(ref.dtype)


def _in_proj_kernel(h_ref, g_ref, w_ref, memkv_ref, a_ref, gate_ref, ymem_ref, w_scr,
                    *, d_a, slab, n_wchunks):
    i = pl.program_id(0)

    @pl.when(i < n_wchunks)
    def _():
        w_scr[i] = w_ref[0].astype(BF16)

    @pl.when(i >= n_wchunks)
    def _():
        na = d_a // W_CHUNK
        for r0 in range(0, h_ref.shape[0], PROJ_ROWS):
            rows = slice(r0, r0 + PROJ_ROWS)
            u = _rms(h_ref[rows, :], g_ref[0]).astype(BF16)
            q_mem = _dot(u, w_scr[2 * na])
            g_mem = _dot(u, w_scr[2 * na + 1])
            for c in range(2 * na):
                dst, col = (a_ref, c) if c < na else (gate_ref, c - na)
                _store_cols(dst, rows, col, _dot(u, w_scr[c]), slab)
                if c < N_MEM_HEADS:
                    _mem_attention_head(c, rows, q_mem, g_mem, memkv_ref, ymem_ref)


def _mem_attention_head(hd, rows, q_mem, g_mem, memkv_ref, ymem_ref):
    sl = slice(hd * MEM_HEAD_DIM, (hd + 1) * MEM_HEAD_DIM)
    q = q_mem[:, sl].astype(BF16)
    k = memkv_ref[0, :, sl]
    v = memkv_ref[0, :, D_MEM + hd * MEM_HEAD_DIM:D_MEM + (hd + 1) * MEM_HEAD_DIM]
    s = _dot_nt(q, k) * (MEM_HEAD_DIM ** -0.5)
    e = jnp.exp(s - jnp.max(s, axis=-1, keepdims=True))
    p = e * (1.0 / jnp.sum(e, axis=-1, keepdims=True))
    o = _dot(p.astype(BF16), v)
    ymem_ref[rows, sl] = (o * _silu(g_mem[:, sl])).astype(ymem_ref.dtype)


def _in_proj(h, norms, w_in, memkv, layer, mem_layer, *, d_a, a_dtype, slab):
    t = h.shape[0]
    n_out = w_in.shape[2]
    assert d_a % W_CHUNK == 0 and D_MEM == W_CHUNK and n_out == 2 * d_a + 2 * D_MEM
    n_wchunks = n_out // W_CHUNK
    n_tiles = t // TM_PROJ

    def tile(i):
        return jnp.maximum(i - n_wchunks, 0)

    if slab:
        a_shape, a_block = (d_a // LANES, t, LANES), (d_a // LANES, TM_PROJ, LANES)
        a_spec = pl.BlockSpec(a_block, lambda i: (0, tile(i), 0))
    else:
        a_shape = (t, d_a)
        a_spec = pl.BlockSpec((TM_PROJ, d_a), lambda i: (tile(i), 0))
    return pl.pallas_call(
        functools.partial(_in_proj_kernel, d_a=d_a, slab=slab, n_wchunks=n_wchunks),
        grid=(n_wchunks + n_tiles,),
        in_specs=[
            pl.BlockSpec((TM_PROJ, D_MODEL), lambda i: (tile(i), 0)),
            _const_spec((1, 1, D_MODEL), (layer, 0, 0)),
            pl.BlockSpec((1, D_MODEL, W_CHUNK),
                         lambda i: (layer, 0, jnp.minimum(i, n_wchunks - 1))),
            _const_spec((1, N_MEM, 2 * D_MEM), (mem_layer, 0, 0)),
        ],
        out_specs=[
            a_spec,
            a_spec,
            pl.BlockSpec((TM_PROJ, D_MEM), lambda i: (tile(i), 0)),
        ],
        out_shape=[
            jax.ShapeDtypeStruct(a_shape, a_dtype),
            jax.ShapeDtypeStruct(a_shape, F32),
            jax.ShapeDtypeStruct((t, D_MEM), BF16),
        ],
        scratch_shapes=[pltpu.VMEM((n_wchunks, D_MODEL, W_CHUNK), BF16)],
        compiler_params=_params(1),
        name="in_proj",
    )(h, norms, w_in, memkv)


def _rglru_gate_products(x, cw, cb, w_r, w_i, xpad_ref):
    tb = x.shape[0]
    xpad_ref[SUBLANES:, :] = x
    xc = x * cw[CONV_WIDTH - 1:CONV_WIDTH, :] + cb
    for k in range(1, CONV_WIDTH):
        xc = xc + (xpad_ref[SUBLANES - k:SUBLANES - k + tb, :]
                   * cw[CONV_WIDTH - 1 - k:CONV_WIDTH - k, :])
    xpad_ref[0:SUBLANES, :] = x[tb - SUBLANES:, :]
    xb = xc.astype(BF16)
    return xc, _dot(xb, w_r), _dot(xb, w_i)


def _rglru_recurrence(xc, pre_r, pre_i, gate, b_r, b_i, half_scale, carry_ref):
    tb = xc.shape[0]
    groups = tb // SUBLANES
    ig = _sigmoid(pre_i + b_i)
    log_a = half_scale * jnp.tanh(0.5 * (pre_r + b_r)) + half_scale
    a = jnp.exp(log_a)
    y2 = log_a + log_a
    e = a * a
    near_one = jnp.where(e == 1.0, -y2, (1.0 - e) * y2 / jnp.log(e))
    u = jnp.sqrt(jnp.where(e > 0.5, near_one, 1.0 - e)) * (ig * xc)

    a3 = a.reshape(groups, SUBLANES, RNN_BLOCK)
    u3 = u.reshape(groups, SUBLANES, RNN_BLOCK)
    row = lax.broadcasted_iota(jnp.int32, (groups, SUBLANES, RNN_BLOCK), 1)
    for k in (1, 2, 4):
        a_sh = pltpu.roll(a3, k, axis=1)
        u_sh = pltpu.roll(u3, k, axis=1)
        m = row >= k
        u3 = jnp.where(m, a3 * u_sh + u3, u3)
        a3 = jnp.where(m, a3 * a_sh, a3)

    carry = carry_ref[...]
    ys = []
    for gi in range(groups):
        hg = u3[gi] + a3[gi] * carry
        ys.append(hg * _silu(gate[gi * SUBLANES:(gi + 1) * SUBLANES, :]))
        carry = jnp.broadcast_to(hg[SUBLANES - 1:SUBLANES, :], (SUBLANES, RNN_BLOCK))
    carry_ref[...] = carry
    return jnp.concatenate(ys, axis=0)


def _rglru_out_kernel(x_ref, g_ref, ym_ref, h_ref, cw_ref, cb_ref, wr_ref, wi_ref, br_ref,
                      bi_ref, lam_ref, w_ref, o_ref, w_scr, wg_scr, xpad_ref, carry_ref,
                      *, n_wchunks):
    i = pl.program_id(0)

    @pl.when(i == 0)
    def _():
        xpad_ref[:, 0:SUBLANES, :] = jnp.zeros((N_RNN_BLOCKS, SUBLANES, RNN_BLOCK), F32)
        carry_ref[...] = jnp.zeros_like(carry_ref)
        wg_scr[0] = wr_ref[0].astype(BF16)
        wg_scr[1] = wi_ref[0].astype(BF16)

    @pl.when(i < n_wchunks)
    def _():
        w_scr[i] = w_ref[0].astype(BF16)

    @pl.when(i >= n_wchunks)
    def _():
        nlam = -lam_ref[0]
        softplus_nlam = jnp.maximum(nlam, 0.0) + jnp.log1p(jnp.exp(-jnp.abs(nlam)))
        half_scale = 0.5 * (-LRU_C * softplus_nlam)
        cw, cb, b_r, b_i = cw_ref[0], cb_ref[0], br_ref[0], bi_ref[0]
        per = W_CHUNK // LANES
        n_groups = N_RNN_BLOCKS // per

        def channels(n):
            return slice(n * RNN_BLOCK, (n + 1) * RNN_BLOCK)

        def gate_products(grp):
            return [_rglru_gate_products(x_ref[n], cw[:, channels(n)], cb[:, channels(n)],
                                         wg_scr[0, n], wg_scr[1, n], xpad_ref.at[n])
                    for n in range(grp * per, (grp + 1) * per)]

        out = h_ref[...]
        products = gate_products(0)
        for grp in range(n_groups):
            upcoming = gate_products(grp + 1) if grp + 1 < n_groups else None
            ys = []
            for k, (xc, pre_r, pre_i) in enumerate(products):
                n = grp * per + k
                y = _rglru_recurrence(xc, pre_r, pre_i, g_ref[n], b_r[:, channels(n)],
                                      b_i[:, channels(n)], half_scale[:, channels(n)],
                                      carry_ref.at[n])
                ys.append(y.astype(BF16))
            out = out + _dot(jnp.concatenate(ys, axis=1), w_scr[grp])
            products = upcoming
        out = out + _dot(ym_ref[...], w_scr[n_wchunks - 1])
        o_ref[...] = out


def _rglru_out(x_rnn, g_rnn, y_mem, h, conv_w, conv_b, w_r, w_i, b_r, b_i, lam, w_out, layer):
    t = h.shape[0]
    n_layers = conv_w.shape[0]
    assert w_out.shape[1] == D_RNN + D_MEM and D_RNN % W_CHUNK == 0 and D_MEM == W_CHUNK
    n_wchunks = w_out.shape[1] // W_CHUNK

    def tile(i):
        return jnp.maximum(i - n_wchunks, 0)

    def rows(p):
        return p.reshape(n_layers, 1, D_RNN)

    slab_spec = pl.BlockSpec((N_RNN_BLOCKS, TM_RNN, RNN_BLOCK), lambda i: (0, tile(i), 0))
    row_spec = _const_spec((1, 1, D_RNN), (layer, 0, 0))
    gate_w_spec = _const_spec((1, N_RNN_BLOCKS, RNN_BLOCK, RNN_BLOCK), (layer, 0, 0, 0))
    return pl.pallas_call(
        functools.partial(_rglru_out_kernel, n_wchunks=n_wchunks),
        grid=(n_wchunks + t // TM_RNN,),
        in_specs=[
            slab_spec,
            slab_spec,
            pl.BlockSpec((TM_RNN, D_MEM), lambda i: (tile(i), 0)),
            pl.BlockSpec((TM_RNN, D_MODEL), lambda i: (tile(i), 0)),
            _const_spec((1, CONV_WIDTH, D_RNN), (layer, 0, 0)),
            row_spec,
            gate_w_spec,
            gate_w_spec,
            row_spec,
            row_spec,
            row_spec,
            pl.BlockSpec((1, W_CHUNK, D_MODEL),
                         lambda i: (layer, jnp.minimum(i, n_wchunks - 1), 0)),
        ],
        out_specs=pl.BlockSpec((TM_RNN, D_MODEL), lambda i: (tile(i), 0)),
        out_shape=jax.ShapeDtypeStruct((t, D_MODEL), F32),
        scratch_shapes=[
            pltpu.VMEM((n_wchunks, W_CHUNK, D_MODEL), BF16),
            pltpu.VMEM((2, N_RNN_BLOCKS, RNN_BLOCK, RNN_BLOCK), BF16),
            pltpu.VMEM((N_RNN_BLOCKS, SUBLANES + TM_RNN, RNN_BLOCK), F32),
            pltpu.VMEM((N_RNN_BLOCKS, SUBLANES, RNN_BLOCK), F32),
        ],
        compiler_params=_params(1),
        name="rglru_out",
    )(x_rnn, g_rnn, y_mem, h, conv_w, rows(conv_b), w_r, w_i, rows(b_r), rows(b_i), rows(lam),
      w_out)


LOG2E = 1.4426950408889634
SB_EXIT_LOG2 = -104.0 * LOG2E
SB_HEADS_PER_STEP = 4


def _sb_block(qs, kvs, lower2, carries, accs, *, diagonal, valid=None):
    tq, tk = qs[0].shape[0], kvs[0][0].shape[0]
    if diagonal:
        mask = (lax.broadcasted_iota(jnp.int32, (tq, tk), 1)
                < lax.broadcasted_iota(jnp.int32, (tq, tk), 0))
    log_betas, log_keeps, split = [], [], []
    for q, (k, _) in zip(qs, kvs):
        z = _dot_nt(q, k) * (SB_HEAD_DIM ** -0.5 * LOG2E)
        l = jnp.log(1.0 + jnp.exp2(-jnp.abs(z))) * LOG2E
        log_beta = jnp.minimum(z, 0.0) - l
        log_keep = log_beta - z
        if diagonal:
            log_keep = jnp.where(mask, log_keep, 0.0)
        hi = log_keep.astype(BF16)
        lo = (log_keep - hi.astype(F32)).astype(BF16)
        log_betas.append(log_beta)
        log_keeps.append(log_keep)
        split.append(jnp.concatenate([hi, lo], axis=1))
    later_all = _dot(jnp.concatenate(split, axis=0), lower2)
    new_carries, new_accs = [], []
    for h, (_, v) in enumerate(kvs):
        x = log_betas[h] + later_all[h * tq:(h + 1) * tq]
        if carries[h] is not None:
            x = x + carries[h]
        w = jnp.exp2(x)
        if diagonal:
            w = jnp.where(mask, w, 0.0)
        if valid is not None:
            v = jnp.where(valid, v, jnp.zeros_like(v))
        new_accs.append(accs[h] + _dot(w.astype(BF16), v))
        total = jnp.sum(log_keeps[h], axis=-1, keepdims=True)
        new_carries.append(total if carries[h] is None else carries[h] + total)
    return new_carries, new_accs


def _sb_attn_kernel(q_ref, g_ref, k_ref, v_ref, o_ref):
    i = pl.program_id(1)
    tq = q_ref.shape[0]
    heads = range(SB_HEADS_PER_STEP)
    lanes = [slice(hd * SB_HEAD_DIM, (hd + 1) * SB_HEAD_DIM) for hd in heads]
    lower = (lax.broadcasted_iota(jnp.int32, (TK_SB, TK_SB), 0)
             > lax.broadcasted_iota(jnp.int32, (TK_SB, TK_SB), 1)).astype(BF16)
    lower2 = jnp.concatenate([lower, lower], axis=0)
    qs = [q_ref[:, sl] for sl in lanes]

    def load_kv(j):
        start = pl.multiple_of(j * TK_SB, TK_SB)
        return [(k_ref[pl.ds(start, TK_SB), sl], v_ref[pl.ds(start, TK_SB), sl])
                for sl in lanes]

    def needs_more(carries):
        worst = functools.reduce(jnp.maximum, carries)
        return (jnp.max(worst) >= SB_EXIT_LOG2).astype(jnp.int32)

    carries, accs = _sb_block(qs, load_kv(i), lower2, [None for _ in heads],
                              [jnp.zeros((tq, SB_HEAD_DIM), F32) for _ in heads],
                              diagonal=True)
    carries, accs = _sb_block(qs, load_kv(jnp.maximum(i - 1, 0)), lower2, carries, accs,
                              diagonal=False, valid=i > 0)

    def cond(state):
        return jnp.logical_and(state[0] >= 0, state[1] > 0)

    def body(state):
        j = state[0]
        carries, accs = _sb_block(qs, load_kv(j), lower2, list(state[2]), list(state[3]),
                                  diagonal=False)
        return j - 1, needs_more(carries), tuple(carries), tuple(accs)

    state = lax.while_loop(cond, body,
                           (i - 2, needs_more(carries), tuple(carries), tuple(accs)))
    accs = state[3]
    for hd in heads:
        o_ref[:, lanes[hd]] = (accs[hd] * _silu(g_ref[:, lanes[hd]])).astype(o_ref.dtype)


def _sb_attn(q_sb, g_sb, kv):
    t = q_sb.shape[0]
    assert TQ_SB == TK_SB
    width = SB_HEADS_PER_STEP * SB_HEAD_DIM
    groups = N_SB_HEADS // SB_HEADS_PER_STEP
    tile_spec = pl.BlockSpec((TQ_SB, width), lambda h, i: (i, h))
    return pl.pallas_call(
        _sb_attn_kernel,
        grid=(groups, t // TQ_SB),
        in_specs=[
            tile_spec,
            tile_spec,
            pl.BlockSpec((t, width), lambda h, i: (0, h)),
            pl.BlockSpec((t, width), lambda h, i: (0, groups + h)),
        ],
        out_specs=tile_spec,
        out_shape=jax.ShapeDtypeStruct((t, D_SB), BF16),
        compiler_params=_params(2),
        name="sb_attn",
    )(q_sb, g_sb, kv, kv)


def _out_proj_kernel(ya_ref, ym_ref, h_ref, w_ref, gf_ref, o_ref, w_scr,
                     *, final_norm, n_wchunks):
    i = pl.program_id(0)

    @pl.when(i < n_wchunks)
    def _():
        w_scr[i] = w_ref[0].astype(BF16)

    @pl.when(i >= n_wchunks)
    def _():
        out = h_ref[...]
        for c in range(n_wchunks - 1):
            out = out + _dot(ya_ref[:, c * W_CHUNK:(c + 1) * W_CHUNK], w_scr[c])
        out = out + _dot(ym_ref[...], w_scr[n_wchunks - 1])
        if final_norm:
            out = _rms(out, gf_ref[...])
        o_ref[...] = out


def _out_proj(ya, ym, h, w_out, layer, g_final, *, final_norm):
    t = h.shape[0]
    d_in = w_out.shape[1]
    d_a = d_in - D_MEM
    assert d_a % W_CHUNK == 0 and D_MEM == W_CHUNK
    n_wchunks = d_in // W_CHUNK

    def tile(i):
        return jnp.maximum(i - n_wchunks, 0)

    return pl.pallas_call(
        functools.partial(_out_proj_kernel, final_norm=final_norm, n_wchunks=n_wchunks),
        grid=(n_wchunks + t // TM_OUT,),
        in_specs=[
            pl.BlockSpec((TM_OUT, d_a), lambda i: (tile(i), 0)),
            pl.BlockSpec((TM_OUT, D_MEM), lambda i: (tile(i), 0)),
            pl.BlockSpec((TM_OUT, D_MODEL), lambda i: (tile(i), 0)),
            pl.BlockSpec((1, W_CHUNK, D_MODEL),
                         lambda i: (layer, jnp.minimum(i, n_wchunks - 1), 0)),
            _const_spec((1, D_MODEL), (0, 0)),
        ],
        out_specs=pl.BlockSpec((TM_OUT, D_MODEL), lambda i: (tile(i), 0)),
        out_shape=jax.ShapeDtypeStruct((t, D_MODEL), F32),
        scratch_shapes=[pltpu.VMEM((n_wchunks, W_CHUNK, D_MODEL), BF16)],
        compiler_params=_params(1),
        name="out_proj",
    )(ya, ym, h, w_out, g_final.reshape(1, D_MODEL))


def _kv_proj_kernel(h_ref, g_ref, w_ref, o_ref, w_scr, *, n_wchunks):
    i = pl.program_id(0)

    @pl.when(i < n_wchunks)
    def _():
        w_scr[i] = w_ref[...].astype(BF16)

    @pl.when(i >= n_wchunks)
    def _():
        for r0 in range(0, h_ref.shape[0], PROJ_ROWS):
            rows = slice(r0, r0 + PROJ_ROWS)
            u = _rms(h_ref[rows, :], g_ref[...]).astype(BF16)
            for c in range(n_wchunks):
                o_ref[rows, c * W_CHUNK:(c + 1) * W_CHUNK] = _dot(u, w_scr[c]).astype(o_ref.dtype)


def _kv_proj(h, g, w_kv):
    t = h.shape[0]
    n_out = w_kv.shape[1]
    assert n_out % W_CHUNK == 0
    n_wchunks = n_out // W_CHUNK

    def tile(i):
        return jnp.maximum(i - n_wchunks, 0)

    return pl.pallas_call(
        functools.partial(_kv_proj_kernel, n_wchunks=n_wchunks),
        grid=(n_wchunks + t // TM_OUT,),
        in_specs=[
            pl.BlockSpec((TM_OUT, D_MODEL), lambda i: (tile(i), 0)),
            _const_spec((1, D_MODEL), (0, 0)),
            pl.BlockSpec((D_MODEL, W_CHUNK), lambda i: (0, jnp.minimum(i, n_wchunks - 1))),
        ],
        out_specs=pl.BlockSpec((TM_OUT, n_out), lambda i: (tile(i), 0)),
        out_shape=jax.ShapeDtypeStruct((t, n_out), BF16),
        scratch_shapes=[pltpu.VMEM((n_wchunks, D_MODEL, W_CHUNK), BF16)],
        compiler_params=_params(1),
        name="kv_proj",
    )(h, g.reshape(1, D_MODEL), w_kv)


def kernel(x, mem, mem_norm, w_mem_kv, norm_a, w_in_a, conv_w, conv_b, w_rec_gate, b_rec_gate,
           w_in_gate, b_in_gate, lru_lambda, w_out_a, kv_norm, w_kv, norm_b, w_in_b, w_out_b,
           final_norm):
    batch = x.shape[0]
    n_a = w_in_a.shape[0]
    n_b = w_in_b.shape[0]
    norms_a = norm_a.reshape(n_a, 1, D_MODEL)
    norms_b = norm_b.reshape(n_b, 1, D_MODEL)
    outs = []
    for b in range(batch):
        memkv = _mem_kv(mem[b], mem_norm, w_mem_kv)
        h = x[b]
        for layer in range(n_a):
            x_rnn, g_rnn, y_mem = _in_proj(h, norms_a, w_in_a, memkv, layer, layer,
                                           d_a=D_RNN, a_dtype=F32, slab=True)
            h = _rglru_out(x_rnn, g_rnn, y_mem, h, conv_w, conv_b, w_rec_gate, w_in_gate,
                           b_rec_gate, b_in_gate, lru_lambda, w_out_a, layer)
        kv = _kv_proj(h, kv_norm, w_kv)
        for j in range(n_b):
            q_sb, g_sb, y_mem = _in_proj(h, norms_b, w_in_b, memkv, j, n_a + j,
                                         d_a=D_SB, a_dtype=BF16, slab=False)
            y_sb = _sb_attn(q_sb, g_sb, kv)
            h = _out_proj(y_sb, y_mem, h, w_out_b, j, final_norm,
                          final_norm=(j == n_b - 1))
        outs.append(h)
    return jnp.stack(outs, axis=0)
```
